```python
import jax, jax.numpy as jnp
from jax import lax
import numpy as np

D_MODEL = 1024
BATCH = 8
SEQ = 2048
DEPTH = 4

D_CONV = D_MODEL
CONV_K = 3
N_HEADS = 16
QK_NOPE = 64
QK_ROPE = 32
V_HEAD = 64
Q_LORA = 768
KV_LORA = 256
ROPE_THETA = 10000.0
Q_BLOCK = 128
SOFTMAX_SCALE = (QK_NOPE + QK_ROPE) ** -0.5
D_FF = 4 * D_MODEL
LN_EPS = 1e-5
RMS_EPS = 1e-6
DEEPNORM_ALPHA = (2 * DEPTH) ** 0.25
DEEPNORM_BETA = (8 * DEPTH) ** -0.25

OFF_B = D_CONV
OFF_C = 2 * D_CONV
OFF_H = 3 * D_CONV
OFF_QA = OFF_H + Q_LORA
OFF_KVA = OFF_QA + KV_LORA
OFF_KR = OFF_KVA + QK_ROPE
IN_PROJ_WIDTH = OFF_KR + 2 * D_MODEL
SPLIT_POINTS = (OFF_B, OFF_C, OFF_H, OFF_QA, OFF_KVA, OFF_KR)

kernel_name = "hybrid_conv_mla_deepnorm_encoder"


def layer_norm(x, g, b):
    xf = x.astype(jnp.float32)
    mu = jnp.mean(xf, axis=-1, keepdims=True)
    xc = xf - mu
    var = jnp.mean(xc * xc, axis=-1, keepdims=True)
    return (xc * lax.rsqrt(var + LN_EPS) * g + b).astype(x.dtype)


def rms_norm(x, g):
    xf = x.astype(jnp.float32)
    ms = jnp.mean(xf * xf, axis=-1, keepdims=True)
    return (xf * lax.rsqrt(ms + RMS_EPS) * g).astype(x.dtype)


def rope_cos_sin(positions, dtype):
    inv_freq = 1.0 / (ROPE_THETA ** (jnp.arange(0, QK_ROPE, 2, dtype=jnp.float32) / QK_ROPE))
    ang = positions.astype(jnp.float32)[..., None] * inv_freq
    return jnp.cos(ang).astype(dtype), jnp.sin(ang).astype(dtype)


def apply_rope(x, cos, sin):
    half = QK_ROPE // 2
    x1, x2 = x[..., :half], x[..., half:]
    return jnp.concatenate([x1 * cos - x2 * sin, x1 * sin + x2 * cos], axis=-1)


def short_conv(u, w):
    up = jnp.pad(u, ((0, 0), (1, 1), (0, 0)))
    return w[0] * up[:, :-2] + w[1] * up[:, 1:-1] + w[2] * up[:, 2:]


def mla_attention(q_nope, q_rope, k_nope, k_rope, v):
    b, s = q_nope.shape[0], q_nope.shape[1]
    nb = s // Q_BLOCK
    qn = q_nope.reshape(b, nb, Q_BLOCK, N_HEADS, QK_NOPE).transpose(1, 0, 2, 3, 4)
    qr = q_rope.reshape(b, nb, Q_BLOCK, N_HEADS, QK_ROPE).transpose(1, 0, 2, 3, 4)

    def one_block(args):
        qn_b, qr_b = args
        scores = (jnp.einsum('bqhd,bkhd->bhqk', qn_b, k_nope)
                  + jnp.einsum('bqhr,bkr->bhqk', qr_b, k_rope))
        p = jax.nn.softmax(scores.astype(jnp.float32) * SOFTMAX_SCALE, axis=-1).astype(v.dtype)
        return jnp.einsum('bhqk,bkhd->bqhd', p, v)

    out = lax.map(one_block, (qn, qr))
    return out.transpose(1, 0, 2, 3, 4).reshape(b, s, N_HEADS * V_HEAD)


def token_mixing(x, cos, sin, w_in, b_gate, conv_w, w_conv_out, q_norm_g, w_q_b,
                 kv_norm_g, w_kv_b, w_mla_o, w_out):
    b, s, _ = x.shape
    proj = x @ w_in
    bg, cg, hc, q_a, kv_a, k_r, gate_logits = jnp.split(proj, SPLIT_POINTS, axis=-1)
    y_conv = (bg * short_conv(cg * hc, conv_w)) @ w_conv_out
    q = (rms_norm(q_a, q_norm_g) @ w_q_b).reshape(b, s, N_HEADS, QK_NOPE + QK_ROPE)
    q_nope = q[..., :QK_NOPE]
    q_rope = apply_rope(q[..., QK_NOPE:], cos[:, :, None, :], sin[:, :, None, :])
    kv = (rms_norm(kv_a, kv_norm_g) @ w_kv_b).reshape(b, s, N_HEADS, QK_NOPE + V_HEAD)
    k_nope, v = kv[..., :QK_NOPE], kv[..., QK_NOPE:]
    k_rope = apply_rope(k_r, cos, sin)
    y_mla = mla_attention(q_nope, q_rope, k_nope, k_rope, v) @ w_mla_o
    g = jax.nn.sigmoid(gate_logits + b_gate)
    g_conv, g_mla = g[..., :D_MODEL], g[..., D_MODEL:]
    return (g_conv * y_conv + g_mla * y_mla) @ w_out


def sq_relu_mlp(x, w_up, w_down):
    h = jax.nn.relu(x @ w_up)
    return (h * h) @ w_down


def setup_inputs(seed: int = 0) -> dict:
    key = jax.random.key(seed)
    ks = jax.random.split(key, 24)
    f32 = jnp.float32

    def nrm(k, shape, scale):
        return jax.random.normal(k, shape, f32) * scale

    def gain(k, shape):
        return 1.0 + 0.02 * jax.random.normal(k, shape, f32)

    L = DEPTH
    x = jax.random.normal(ks[0], (BATCH, SEQ, D_MODEL), f32)
    offsets = jax.random.randint(ks[1], (BATCH, 1), 0, SEQ, dtype=jnp.int32)
    positions = offsets + jnp.arange(SEQ, dtype=jnp.int32)[None, :]
    return {
        "x": x,
        "positions": positions,
        "ln_in_g": gain(ks[2], (D_MODEL,)),
        "ln_in_b": nrm(ks[3], (D_MODEL,), 0.02),
        "w_in": nrm(ks[4], (L, D_MODEL, IN_PROJ_WIDTH), D_MODEL ** -0.5),
        "b_gate": nrm(ks[5], (L, 2 * D_MODEL), 0.02),
        "conv_w": nrm(ks[6], (L, CONV_K, D_CONV), CONV_K ** -0.5),
        "w_conv_out": nrm(ks[7], (L, D_CONV, D_MODEL), D_CONV ** -0.5),
        "q_norm_g": gain(ks[8], (L, Q_LORA)),
        "w_q_b": nrm(ks[9], (L, Q_LORA, N_HEADS * (QK_NOPE + QK_ROPE)), Q_LORA ** -0.5),
        "kv_norm_g": gain(ks[10], (L, KV_LORA)),
        "w_kv_b": nrm(ks[11], (L, KV_LORA, N_HEADS * (QK_NOPE + V_HEAD)), KV_LORA ** -0.5),
        "w_mla_o": nrm(ks[12], (L, N_HEADS * V_HEAD, D_MODEL), (N_HEADS * V_HEAD) ** -0.5),
        "w_out": nrm(ks[13], (L, D_MODEL, D_MODEL), DEEPNORM_BETA * D_MODEL ** -0.5),
        "ln_mix_g": gain(ks[14], (L, D_MODEL)),
        "ln_mix_b": nrm(ks[15], (L, D_MODEL), 0.02),
        "w_up": nrm(ks[16], (L, D_MODEL, D_FF), D_MODEL ** -0.5),
        "w_down": nrm(ks[17], (L, D_FF, D_MODEL), DEEPNORM_BETA * D_FF ** -0.5),
        "ln_ffn_g": gain(ks[18], (L, D_MODEL)),
        "ln_ffn_b": nrm(ks[19], (L, D_MODEL), 0.02),
    }


def reference(x, positions, ln_in_g, ln_in_b, w_in, b_gate, conv_w, w_conv_out,
              q_norm_g, w_q_b, kv_norm_g, w_kv_b, w_mla_o, w_out, ln_mix_g, ln_mix_b,
              w_up, w_down, ln_ffn_g, ln_ffn_b):
    cos, sin = rope_cos_sin(positions, x.dtype)
    h = layer_norm(x, ln_in_g, ln_in_b)
    for l in range(DEPTH):
        mix = token_mixing(h, cos, sin, w_in[l], b_gate[l], conv_w[l], w_conv_out[l],
                           q_norm_g[l], w_q_b[l], kv_norm_g[l], w_kv_b[l], w_mla_o[l], w_out[l])
        h = layer_norm(DEEPNORM_ALPHA * h + mix, ln_mix_g[l], ln_mix_b[l])
        h = layer_norm(DEEPNORM_ALPHA * h + sq_relu_mlp(h, w_up[l], w_down[l]), ln_ffn_g[l], ln_ffn_b[l])
    return h
```

```python
import functools
import math

import jax
import jax.numpy as jnp
from jax import lax
from jax.experimental import pallas as pl
from jax.experimental.pallas import tpu as pltpu

D_MODEL = 1024
N_HEADS = 16
QK_NOPE = 64
QK_ROPE = 32
V_HEAD = 64
Q_LORA = 768
KV_LORA = 256
ROPE_THETA = 10000.0
D_FF = 4 * D_MODEL
LN_EPS = 1e-5
RMS_EPS = 1e-6
HEAD_PAD = 128

OFF_B = D_MODEL
OFF_C = 2 * D_MODEL
OFF_H = 3 * D_MODEL
OFF_QA = OFF_H + Q_LORA
OFF_KVA = OFF_QA + KV_LORA
OFF_KR = OFF_KVA + QK_ROPE

VMEM_LIMIT_BYTES = 56 * 1024 * 1024

BF16 = jnp.bfloat16
F32 = jnp.float32

_NT = (((1,), (1,)), ((), ()))


def _dot(a, b):
    return jnp.dot(a, b, preferred_element_type=F32)


def _layer_norm(z, g, b):
    mu = jnp.mean(z, axis=-1, keepdims=True)
    zc = z - mu
    var = jnp.mean(zc * zc, axis=-1, keepdims=True)
    return zc * lax.rsqrt(var + LN_EPS) * g + b


def _rms_norm(z, g):
    ms = jnp.mean(z * z, axis=-1, keepdims=True)
    return z * lax.rsqrt(ms + RMS_EPS) * g


def _const_spec(shape):
    nd = len(shape)
    return pl.BlockSpec(shape, lambda *_: (0,) * nd, pipeline_mode=pl.Buffered(1))


def _params():
    return pltpu.CompilerParams(
        dimension_semantics=("arbitrary",), vmem_limit_bytes=VMEM_LIMIT_BYTES)


def _rope_table_kernel(pos_ref, freq_ref, cos_ref, sin_ref):
    ang = pos_ref[...].astype(F32) * freq_ref[...]
    cos_ref[...] = jnp.cos(ang)
    sin_ref[...] = jnp.sin(ang)


def _rope_tables(pos_b, freq_row, tm):
    t = pos_b.shape[0]
    row = pl.BlockSpec((tm, HEAD_PAD), lambda i: (i, 0))
    return pl.pallas_call(
        _rope_table_kernel,
        out_shape=(jax.ShapeDtypeStruct((t, HEAD_PAD), F32),) * 2,
        grid=(t // tm,),
        in_specs=[row, _const_spec((1, HEAD_PAD))],
        out_specs=(row, row),
        compiler_params=_params(),
        name="rope_tables",
    )(pos_b, freq_row)


def _ln_kernel(x_ref, g_ref, b_ref, o_ref):
    o_ref[...] = _layer_norm(x_ref[...], g_ref[...], b_ref[...])


def _entry_norm(x2, g, b, tm):
    t, d = x2.shape
    row = pl.BlockSpec((tm, d), lambda i: (i, 0))
    return pl.pallas_call(
        _ln_kernel,
        out_shape=jax.ShapeDtypeStruct((t, d), F32),
        grid=(t // tm,),
        in_specs=[row, _const_spec((1, d)), _const_spec((1, d))],
        out_specs=row,
        compiler_params=_params(),
        name="entry_norm",
    )(x2, g, b)


def _proj_kernel(q_scale, h_ref, cos_ref, sin_ref, wc_ref, wh_ref, wqa_ref, wkva_ref,
                 wkr_ref, qg_ref, wqb_ref, kvg_ref, wk_ref, wvt_ref,
                 u_ref, q_ref, k_ref, vt_ref):
    x = h_ref[...].astype(BF16)
    cos = cos_ref[...]
    sin = sin_ref[...]

    u_ref[...] = (_dot(x, wc_ref[...]) * _dot(x, wh_ref[...])).astype(BF16)

    lane = lax.broadcasted_iota(jnp.int32, cos.shape, 1)
    mq = jnp.where(lane < QK_NOPE, 1.0, jnp.where(lane < QK_NOPE + QK_ROPE, cos, sin)) * q_scale
    mq2 = jnp.concatenate([mq, mq], axis=1)
    qn = _rms_norm(_dot(x, wqa_ref[...]), qg_ref[...]).astype(BF16)
    for hp in range(N_HEADS // 2):
        sl = slice(hp * 2 * HEAD_PAD, (hp + 1) * 2 * HEAD_PAD)
        q_ref[:, sl] = (_dot(qn, wqb_ref[:, sl]) * mq2).astype(BF16)

    kr2 = _dot(x, wkr_ref[...])
    rk = kr2[:, :HEAD_PAD] * cos + kr2[:, HEAD_PAD:] * sin
    rk2 = jnp.concatenate([rk, rk], axis=1)
    kvn = _rms_norm(_dot(x, wkva_ref[...]), kvg_ref[...]).astype(BF16)
    for hp in range(N_HEADS // 2):
        sl = slice(hp * 2 * HEAD_PAD, (hp + 1) * 2 * HEAD_PAD)
        k_ref[:, sl] = (_dot(kvn, wk_ref[:, sl]) + rk2).astype(BF16)

    vt_ref[0] = lax.dot_general(wvt_ref[...], kvn, _NT,
                                preferred_element_type=F32).astype(BF16)


def _proj(h, cos_t, sin_t, w, seq, tm):
    t, d = h.shape
    nb = seq // tm
    hq = N_HEADS * HEAD_PAD
    hv = N_HEADS * V_HEAD
    row = lambda width: pl.BlockSpec((tm, width), lambda i: (i, 0))
    weights = (w["wc"], w["wh"], w["wqa"], w["wkva"], w["wkr"], w["qg"], w["wqb"],
               w["kvg"], w["wk"], w["wvt"])
    q_scale = (QK_NOPE + QK_ROPE) ** -0.5 * math.log2(math.e)
    return pl.pallas_call(
        functools.partial(_proj_kernel, q_scale),
        out_shape=(jax.ShapeDtypeStruct((t, d), BF16),
                   jax.ShapeDtypeStruct((t, hq), BF16),
                   jax.ShapeDtypeStruct((t, hq), BF16),
                   jax.ShapeDtypeStruct((t // seq, hv, seq), BF16)),
        grid=(t // tm,),
        in_specs=[row(d), row(HEAD_PAD), row(HEAD_PAD)] + [_const_spec(a.shape) for a in weights],
        out_specs=(row(d), row(hq), row(hq),
                   pl.BlockSpec((1, hv, tm), lambda i: (i // nb, 0, i % nb))),
        compiler_params=_params(),
        name="proj",
    )(h, cos_t, sin_t, *weights)


def _attn_kernel(q_ref, k_ref, vt_ref, y_ref):
    for hp in range(N_HEADS // 2):
        outs = []
        for j in range(2):
            h = 2 * hp + j
            q = q_ref[:, h * HEAD_PAD:(h + 1) * HEAD_PAD]
            k = k_ref[:, h * HEAD_PAD:(h + 1) * HEAD_PAD]
            st = lax.dot_general(k, q, _NT, preferred_element_type=F32)
            m = jnp.max(st, axis=0, keepdims=True)
            p = jnp.exp2(st - m)
            l = jnp.sum(p, axis=0, keepdims=True)
            ot = _dot(vt_ref[0, h * V_HEAD:(h + 1) * V_HEAD, :], p.astype(BF16))
            outs.append(ot / l)
        o2 = jnp.concatenate(outs, axis=0)
        y_ref[:, hp * 2 * V_HEAD:(hp + 1) * 2 * V_HEAD] = o2.T.astype(BF16)


def _attention(q, k, vt, seq, tq):
    t, hq = q.shape
    nb = seq // tq
    hv = N_HEADS * V_HEAD
    return pl.pallas_call(
        _attn_kernel,
        out_shape=jax.ShapeDtypeStruct((t, hv), BF16),
        grid=(t // tq,),
        in_specs=[pl.BlockSpec((tq, hq), lambda i: (i, 0)),
                  pl.BlockSpec((seq, hq), lambda i: (i // nb, 0)),
                  pl.BlockSpec((1, hv, seq), lambda i: (i // nb, 0, 0))],
        out_specs=pl.BlockSpec((tq, hv), lambda i: (i, 0)),
        compiler_params=_params(),
        name="attention",
    )(q, k, vt)


def _mix_kernel(alpha, nb, h_ref, u_ref, up_ref, un_ref, y_ref, wb_ref, wg_ref, bg_ref,
                cw_ref, wco_ref, wmo_ref, wo_ref, lg_ref, lb_ref, o_ref):
    i = pl.program_id(0)
    hres = h_ref[...]
    x = hres.astype(BF16)
    tm = hres.shape[0]

    u = u_ref[...].astype(F32)
    halo = up_ref.shape[0]
    has_prev = (i % nb != 0).astype(F32)
    has_next = (i % nb != nb - 1).astype(F32)
    prev_row = up_ref[halo - 1:halo, :].astype(F32) * has_prev
    next_row = un_ref[0:1, :].astype(F32) * has_next
    r = lax.broadcasted_iota(jnp.int32, u.shape, 0)
    u_m1 = jnp.where(r == 0, prev_row, pltpu.roll(u, 1, axis=0))
    u_p1 = jnp.where(r == tm - 1, next_row, pltpu.roll(u, tm - 1, axis=0))
    cw = cw_ref[...]
    conv = cw[0:1, :] * u_m1 + cw[1:2, :] * u + cw[2:3, :] * u_p1

    yc = _dot((_dot(x, wb_ref[...]) * conv).astype(BF16), wco_ref[...])
    ym = _dot(y_ref[...], wmo_ref[...])
    g = jax.nn.sigmoid(_dot(x, wg_ref[...]) + bg_ref[...])
    merged = (g[:, :D_MODEL] * yc + g[:, D_MODEL:] * ym).astype(BF16)
    mix = _dot(merged, wo_ref[...])
    o_ref[...] = _layer_norm(alpha * hres + mix, lg_ref[...], lb_ref[...])


def _mix(h, u, y, w, alpha, seq, tm):
    t, d = h.shape
    nb = seq // tm
    halo = 16
    hb = tm // halo
    nhalo = t // halo
    row = lambda width: pl.BlockSpec((tm, width), lambda i: (i, 0))
    weights = (w["wb"], w["wg"], w["bgate"], w["convw"], w["wco"], w["wmo"], w["wout"],
               w["lmg"], w["lmb"])
    return pl.pallas_call(
        functools.partial(_mix_kernel, alpha, nb),
        out_shape=jax.ShapeDtypeStruct((t, d), F32),
        grid=(t // tm,),
        in_specs=[row(d), row(d),
                  pl.BlockSpec((halo, d), lambda i: (jnp.maximum(i * hb - 1, 0), 0)),
                  pl.BlockSpec((halo, d), lambda i: (jnp.minimum((i + 1) * hb, nhalo - 1), 0)),
                  row(d)] + [_const_spec(a.shape) for a in weights],
        out_specs=row(d),
        compiler_params=_params(),
        name="mix",
    )(h, u, u, u, y, *weights)


def _mlp_kernel(alpha, n_chunks, h_ref, wu_ref, wd_ref, lg_ref, lb_ref, o_ref):
    hres = h_ref[...]
    x = hres.astype(BF16)
    ck = D_FF // n_chunks
    acc = None
    for c in range(n_chunks):
        a = jnp.maximum(_dot(x, wu_ref[:, c * ck:(c + 1) * ck]), 0.0)
        part = _dot((a * a).astype(BF16), wd_ref[c * ck:(c + 1) * ck, :])
        acc = part if acc is None else acc + part
    o_ref[...] = _layer_norm(alpha * hres + acc, lg_ref[...], lb_ref[...])


def _mlp(h, w, alpha, tm):
    t, d = h.shape
    row = pl.BlockSpec((tm, d), lambda i: (i, 0))
    weights = (w["wup"], w["wdown"], w["lfg"], w["lfb"])
    return pl.pallas_call(
        functools.partial(_mlp_kernel, alpha, 4),
        out_shape=jax.ShapeDtypeStruct((t, d), F32),
        grid=(t // tm,),
        in_specs=[row] + [_const_spec(a.shape) for a in weights],
        out_specs=row,
        compiler_params=_params(),
        name="mlp",
    )(h, *weights)


def _rot_cols(w):
    half = QK_ROPE // 2
    return jnp.concatenate([-w[..., half:], w[..., :half]], axis=-1)


def _prep_layer(l, w_in, b_gate, conv_w, w_conv_out, q_norm_g, w_q_b, kv_norm_g, w_kv_b,
                w_mla_o, w_out, ln_mix_g, ln_mix_b, w_up, w_down, ln_ffn_g, ln_ffn_b):
    win = w_in[l]
    d = D_MODEL
    wy = win[:, OFF_KVA:OFF_KR]
    z64 = jnp.zeros((d, QK_NOPE), F32)
    wkr = jnp.concatenate([z64, wy, wy, z64, _rot_cols(wy), _rot_cols(wy)], axis=1)

    wq = w_q_b[l].reshape(Q_LORA, N_HEADS, QK_NOPE + QK_ROPE)
    wq_rope = wq[..., QK_NOPE:]
    wqb = jnp.concatenate([wq, _rot_cols(wq_rope)], axis=-1).reshape(Q_LORA, N_HEADS * HEAD_PAD)

    wkv = w_kv_b[l].reshape(KV_LORA, N_HEADS, QK_NOPE + V_HEAD)
    wk = jnp.concatenate([wkv[..., :QK_NOPE], jnp.zeros_like(wkv[..., QK_NOPE:])], axis=-1)
    wk = wk.reshape(KV_LORA, N_HEADS * HEAD_PAD)
    wvt = wkv[..., QK_NOPE:].reshape(KV_LORA, N_HEADS * V_HEAD).T

    bf = lambda a: a.astype(BF16)
    row = lambda a: a.reshape(1, -1)
    return {
        "wb": bf(win[:, :OFF_B]), "wc": bf(win[:, OFF_B:OFF_C]), "wh": bf(win[:, OFF_C:OFF_H]),
        "wqa": bf(win[:, OFF_H:OFF_QA]), "wkva": bf(win[:, OFF_QA:OFF_KVA]), "wkr": bf(wkr),
        "wg": bf(win[:, OFF_KR:]), "bgate": row(b_gate[l]),
        "qg": row(q_norm_g[l]), "wqb": bf(wqb), "kvg": row(kv_norm_g[l]),
        "wk": bf(wk), "wvt": bf(wvt),
        "convw": conv_w[l], "wco": bf(w_conv_out[l]), "wmo": bf(w_mla_o[l]), "wout": bf(w_out[l]),
        "lmg": row(ln_mix_g[l]), "lmb": row(ln_mix_b[l]),
        "wup": bf(w_up[l]), "wdown": bf(w_down[l]),
        "lfg": row(ln_ffn_g[l]), "lfb": row(ln_ffn_b[l]),
    }


def kernel(x, positions, ln_in_g, ln_in_b, w_in, b_gate, conv_w, w_conv_out, q_norm_g, w_q_b, kv_norm_g, w_kv_b, w_mla_o, w_out, ln_mix_g, ln_mix_b, w_up, w_down, ln_ffn_g, ln_ffn_b):
    batch, seq, d = x.shape
    depth = w_in.shape[0]
    t = batch * seq
    alpha = (2 * depth) ** 0.25
    tm = 512
    tq = 256

    inv_freq = 1.0 / (ROPE_THETA ** (jnp.arange(0, QK_ROPE, 2, dtype=F32) / QK_ROPE))
    freq_row = jnp.concatenate([jnp.zeros((QK_NOPE,), F32)] + [inv_freq] * 4).reshape(1, HEAD_PAD)
    pos_b = jnp.broadcast_to(positions.reshape(t, 1), (t, HEAD_PAD))
    cos_t, sin_t = _rope_tables(pos_b, freq_row, tm)

    h = _entry_norm(x.reshape(t, d), ln_in_g.reshape(1, d), ln_in_b.reshape(1, d), tm)
    for l in range(depth):
        w = _prep_layer(l, w_in, b_gate, conv_w, w_conv_out, q_norm_g, w_q_b, kv_norm_g,
                        w_kv_b, w_mla_o, w_out, ln_mix_g, ln_mix_b, w_up, w_down,
                        ln_ffn_g, ln_ffn_b)
        u, q, k, vt = _proj(h, cos_t, sin_t, w, seq, tm)
        y = _attention(q, k, vt, seq, tq)
        h = _mix(h, u, y, w, alpha, seq, tm)
        h = _mlp(h, w, alpha, tm)
    return h.reshape(batch, seq, d)
```

```python
import functools
import math

import jax
import jax.numpy as jnp
from jax import lax
from jax.experimental import pallas as pl
from jax.experimental.pallas import tpu as pltpu

D_MODEL = 1024
N_HEADS = 16
QK_NOPE = 64
QK_ROPE = 32
V_HEAD = 64
Q_LORA = 768
KV_LORA = 256
ROPE_THETA = 10000.0
D_FF = 4 * D_MODEL
LN_EPS = 1e-5
RMS_EPS = 1e-6
HEAD_PAD = 128
SUBLANES = 8
BF16_SUBLANES = 16

OFF_B = D_MODEL
OFF_C = 2 * D_MODEL
OFF_H = 3 * D_MODEL
OFF_QA = OFF_H + Q_LORA
OFF_KVA = OFF_QA + KV_LORA
OFF_KR = OFF_KVA + QK_ROPE

VMEM_LIMIT_BYTES = 56 * 1024 * 1024

BF16 = jnp.bfloat16
F32 = jnp.float32

_NT = (((1,), (1,)), ((), ()))


def _dot(a, b):
    return jnp.dot(a, b, preferred_element_type=F32)


def _layer_norm(z, g, b):
    mu = jnp.mean(z, axis=-1, keepdims=True)
    zc = z - mu
    var = jnp.mean(zc * zc, axis=-1, keepdims=True)
    return zc * lax.rsqrt(var + LN_EPS) * g + b


def _rms_norm(z, g):
    ms = jnp.mean(z * z, axis=-1, keepdims=True)
    return z * lax.rsqrt(ms + RMS_EPS) * g


def _const_spec(shape):
    nd = len(shape)
    return pl.BlockSpec(shape, lambda *_: (0,) * nd, pipeline_mode=pl.Buffered(1))


def _params():
    return pltpu.CompilerParams(
        dimension_semantics=("arbitrary",), vmem_limit_bytes=VMEM_LIMIT_BYTES)


def _rope_table_kernel(pos_ref, freq_ref, cos_ref, sin_ref):
    ang = pos_ref[...].astype(F32) * freq_ref[...]
    cos_ref[...] = jnp.cos(ang)
    sin_ref[...] = jnp.sin(ang)


def _rope_tables(pos_b, freq_row, tm):
    t = pos_b.shape[0]
    row = pl.BlockSpec((tm, HEAD_PAD), lambda i: (i, 0))
    return pl.pallas_call(
        _rope_table_kernel,
        out_shape=(jax.ShapeDtypeStruct((t, HEAD_PAD), F32),) * 2,
        grid=(t // tm,),
        in_specs=[row, _const_spec((1, HEAD_PAD))],
        out_specs=(row, row),
        compiler_params=_params(),
        name="rope_tables",
    )(pos_b, freq_row)


def _ln_kernel(x_ref, g_ref, b_ref, o_ref):
    o_ref[...] = _layer_norm(x_ref[...], g_ref[...], b_ref[...])


def _entry_norm(x2, g, b, tm):
    t, d = x2.shape
    row = pl.BlockSpec((tm, d), lambda i: (i, 0))
    return pl.pallas_call(
        _ln_kernel,
        out_shape=jax.ShapeDtypeStruct((t, d), F32),
        grid=(t // tm,),
        in_specs=[row, _const_spec((1, d)), _const_spec((1, d))],
        out_specs=row,
        compiler_params=_params(),
        name="entry_norm",
    )(x2, g, b)


def _proj_kernel(q_scale, h_ref, cos_ref, sin_ref, wc_ref, wh_ref, wqa_ref, wkva_ref,
                 wkr_ref, qg_ref, wqb_ref, kvg_ref, wk_ref, wvt_ref,
                 u_ref, q_ref, k_ref, vt_ref):
    x = h_ref[...].astype(BF16)
    cos = cos_ref[...]
    sin = sin_ref[...]

    u_ref[...] = (_dot(x, wc_ref[...]) * _dot(x, wh_ref[...])).astype(BF16)

    lane = lax.broadcasted_iota(jnp.int32, cos.shape, 1)
    mq = jnp.where(lane < QK_NOPE, 1.0, jnp.where(lane < QK_NOPE + QK_ROPE, cos, sin)) * q_scale
    mq2 = jnp.concatenate([mq, mq], axis=1)
    qn = _rms_norm(_dot(x, wqa_ref[...]), qg_ref[...]).astype(BF16)
    for hp in range(N_HEADS // 2):
        sl = slice(hp * 2 * HEAD_PAD, (hp + 1) * 2 * HEAD_PAD)
        q_ref[:, sl] = (_dot(qn, wqb_ref[:, sl]) * mq2).astype(BF16)

    kr2 = _dot(x, wkr_ref[...])
    rk = kr2[:, :HEAD_PAD] * cos + kr2[:, HEAD_PAD:] * sin
    rk2 = jnp.concatenate([rk, rk], axis=1)
    kvn = _rms_norm(_dot(x, wkva_ref[...]), kvg_ref[...]).astype(BF16)
    for hp in range(N_HEADS // 2):
        sl = slice(hp * 2 * HEAD_PAD, (hp + 1) * 2 * HEAD_PAD)
        k_ref[:, sl] = (_dot(kvn, wk_ref[:, sl]) + rk2).astype(BF16)

    vt_ref[0] = lax.dot_general(wvt_ref[...], kvn, _NT,
                                preferred_element_type=F32).astype(BF16)


def _proj(h, cos_t, sin_t, w, seq, tm):
    t, d = h.shape
    nb = seq // tm
    hq = N_HEADS * HEAD_PAD
    hv = N_HEADS * V_HEAD
    row = lambda width: pl.BlockSpec((tm, width), lambda i: (i, 0))
    weights = (w["wc"], w["wh"], w["wqa"], w["wkva"], w["wkr"], w["qg"], w["wqb"],
               w["kvg"], w["wk"], w["wvt"])
    q_scale = (QK_NOPE + QK_ROPE) ** -0.5 * math.log2(math.e)
    return pl.pallas_call(
        functools.partial(_proj_kernel, q_scale),
        out_shape=(jax.ShapeDtypeStruct((t, d), BF16),
                   jax.ShapeDtypeStruct((t, hq), BF16),
                   jax.ShapeDtypeStruct((t, hq), BF16),
                   jax.ShapeDtypeStruct((t // seq, hv, seq), BF16)),
        grid=(t // tm,),
        in_specs=[row(d), row(HEAD_PAD), row(HEAD_PAD)] + [_const_spec(a.shape) for a in weights],
        out_specs=(row(d), row(hq), row(hq),
                   pl.BlockSpec((1, hv, tm), lambda i: (i // nb, 0, i % nb))),
        compiler_params=_params(),
        name="proj",
    )(h, cos_t, sin_t, *weights)


def _row_max8(x):
    while x.shape[0] > SUBLANES:
        half = x.shape[0] // 2
        x = jnp.maximum(x[:half], x[half:])
    return x


def _attn_kernel(kc, q_ref, k_ref, vt_ref, y_ref, s_buf, p_buf):
    seq = k_ref.shape[0]
    nc = seq // kc
    ones = jnp.ones((BF16_SUBLANES, kc), BF16)
    part_max, col_max, acc, outs = {}, {}, {}, {}

    def slot(h):
        return h % 2

    def score_step(h, c):
        q = q_ref[:, h * HEAD_PAD:(h + 1) * HEAD_PAD]
        k = k_ref[c * kc:(c + 1) * kc, h * HEAD_PAD:(h + 1) * HEAD_PAD]
        st = lax.dot_general(k, q, _NT, preferred_element_type=F32)
        s_buf[slot(h), c * kc:(c + 1) * kc, :] = st
        cm = _row_max8(st)
        part_max[h] = cm if c == 0 else jnp.maximum(part_max[h], cm)
        if c == nc - 1:
            col_max[h] = jnp.max(part_max.pop(h), axis=0, keepdims=True)

    def exp_step(h, c):
        s = s_buf[slot(h), c * kc:(c + 1) * kc, :]
        p_buf[slot(h), c * kc:(c + 1) * kc, :] = jnp.exp2(s - col_max[h]).astype(BF16)

    def value_step(h, c):
        v1 = jnp.concatenate(
            [vt_ref[0, h * V_HEAD:(h + 1) * V_HEAD, c * kc:(c + 1) * kc], ones], axis=0)
        part = _dot(v1, p_buf[slot(h), c * kc:(c + 1) * kc, :])
        acc[h] = part if c == 0 else acc[h] + part
        if c == nc - 1:
            ot = acc.pop(h)
            outs[h] = ot[:V_HEAD] / ot[V_HEAD:V_HEAD + 1]
            if h % 2 == 1:
                o2 = jnp.concatenate([outs.pop(h - 1), outs.pop(h)], axis=0)
                y_ref[:, (h - 1) * V_HEAD:(h + 1) * V_HEAD] = o2.T.astype(BF16)

    lag = 1
    total = N_HEADS * nc
    for g in range(total + 2 * (nc + lag)):
        for stage, step in enumerate((score_step, exp_step, value_step)):
            gs = g - stage * (nc + lag)
            if 0 <= gs < total:
                step(gs // nc, gs % nc)


def _attention(q, k, vt, seq, tq, kc):
    t, hq = q.shape
    nb = seq // tq
    hv = N_HEADS * V_HEAD
    return pl.pallas_call(
        functools.partial(_attn_kernel, kc),
        out_shape=jax.ShapeDtypeStruct((t, hv), BF16),
        grid=(t // tq,),
        in_specs=[pl.BlockSpec((tq, hq), lambda i: (i, 0)),
                  pl.BlockSpec((seq, hq), lambda i: (i // nb, 0)),
                  pl.BlockSpec((1, hv, seq), lambda i: (i // nb, 0, 0))],
        out_specs=pl.BlockSpec((tq, hv), lambda i: (i, 0)),
        scratch_shapes=[pltpu.VMEM((2, seq, tq), F32), pltpu.VMEM((2, seq, tq), BF16)],
        compiler_params=_params(),
        name="attention",
    )(q, k, vt)


def _mix_kernel(alpha, nb, h_ref, u_ref, up_ref, un_ref, y_ref, wb_ref, wg_ref, bg_ref,
                cw_ref, wco_ref, wmo_ref, wo_ref, lg_ref, lb_ref, o_ref):
    i = pl.program_id(0)
    hres = h_ref[...]
    x = hres.astype(BF16)
    tm = hres.shape[0]

    u = u_ref[...].astype(F32)
    halo = up_ref.shape[0]
    has_prev = (i % nb != 0).astype(F32)
    has_next = (i % nb != nb - 1).astype(F32)
    prev_row = up_ref[halo - 1:halo, :].astype(F32) * has_prev
    next_row = un_ref[0:1, :].astype(F32) * has_next
    r = lax.broadcasted_iota(jnp.int32, u.shape, 0)
    u_m1 = jnp.where(r == 0, prev_row, pltpu.roll(u, 1, axis=0))
    u_p1 = jnp.where(r == tm - 1, next_row, pltpu.roll(u, tm - 1, axis=0))
    cw = cw_ref[...]
    conv = cw[0:1, :] * u_m1 + cw[1:2, :] * u + cw[2:3, :] * u_p1

    yc = _dot((_dot(x, wb_ref[...]) * conv).astype(BF16), wco_ref[...])
    ym = _dot(y_ref[...], wmo_ref[...])
    g = jax.nn.sigmoid(_dot(x, wg_ref[...]) + bg_ref[...])
    merged = (g[:, :D_MODEL] * yc + g[:, D_MODEL:] * ym).astype(BF16)
    mix = _dot(merged, wo_ref[...])
    o_ref[...] = _layer_norm(alpha * hres + mix, lg_ref[...], lb_ref[...])


def _mix(h, u, y, w, alpha, seq, tm):
    t, d = h.shape
    nb = seq // tm
    halo = 16
    hb = tm // halo
    nhalo = t // halo
    row = lambda width: pl.BlockSpec((tm, width), lambda i: (i, 0))
    weights = (w["wb"], w["wg"], w["bgate"], w["convw"], w["wco"], w["wmo"], w["wout"],
               w["lmg"], w["lmb"])
    return pl.pallas_call(
        functools.partial(_mix_kernel, alpha, nb),
        out_shape=jax.ShapeDtypeStruct((t, d), F32),
        grid=(t // tm,),
        in_specs=[row(d), row(d),
                  pl.BlockSpec((halo, d), lambda i: (jnp.maximum(i * hb - 1, 0), 0)),
                  pl.BlockSpec((halo, d), lambda i: (jnp.minimum((i + 1) * hb, nhalo - 1), 0)),
                  row(d)] + [_const_spec(a.shape) for a in weights],
        out_specs=row(d),
        compiler_params=_params(),
        name="mix",
    )(h, u, u, u, y, *weights)


def _mlp_kernel(alpha, n_chunks, h_ref, wu_ref, wd_ref, lg_ref, lb_ref, o_ref):
    hres = h_ref[...]
    x = hres.astype(BF16)
    ck = D_FF // n_chunks
    acc = None
    for c in range(n_chunks):
        a = jnp.maximum(_dot(x, wu_ref[:, c * ck:(c + 1) * ck]), 0.0)
        part = _dot((a * a).astype(BF16), wd_ref[c * ck:(c + 1) * ck, :])
        acc = part if acc is None else acc + part
    o_ref[...] = _layer_norm(alpha * hres + acc, lg_ref[...], lb_ref[...])


def _mlp(h, w, alpha, tm):
    t, d = h.shape
    row = pl.BlockSpec((tm, d), lambda i: (i, 0))
    weights = (w["wup"], w["wdown"], w["lfg"], w["lfb"])
    return pl.pallas_call(
        functools.partial(_mlp_kernel, alpha, 4),
        out_shape=jax.ShapeDtypeStruct((t, d), F32),
        grid=(t // tm,),
        in_specs=[row] + [_const_spec(a.shape) for a in weights],
        out_specs=row,
        compiler_params=_params(),
        name="mlp",
    )(h, *weights)


def _rot_cols(w):
    half = QK_ROPE // 2
    return jnp.concatenate([-w[..., half:], w[..., :half]], axis=-1)


def _prep_layer(l, w_in, b_gate, conv_w, w_conv_out, q_norm_g, w_q_b, kv_norm_g, w_kv_b,
                w_mla_o, w_out, ln_mix_g, ln_mix_b, w_up, w_down, ln_ffn_g, ln_ffn_b):
    win = w_in[l]
    d = D_MODEL
    wy = win[:, OFF_KVA:OFF_KR]
    z64 = jnp.zeros((d, QK_NOPE), F32)
    wkr = jnp.concatenate([z64, wy, wy, z64, _rot_cols(wy), _rot_cols(wy)], axis=1)

    wq = w_q_b[l].reshape(Q_LORA, N_HEADS, QK_NOPE + QK_ROPE)
    wq_rope = wq[..., QK_NOPE:]
    wqb = jnp.concatenate([wq, _rot_cols(wq_rope)], axis=-1).reshape(Q_LORA, N_HEADS * HEAD_PAD)

    wkv = w_kv_b[l].reshape(KV_LORA, N_HEADS, QK_NOPE + V_HEAD)
    wk = jnp.concatenate([wkv[..., :QK_NOPE], jnp.zeros_like(wkv[..., QK_NOPE:])], axis=-1)
    wk = wk.reshape(KV_LORA, N_HEADS * HEAD_PAD)
    wvt = wkv[..., QK_NOPE:].reshape(KV_LORA, N_HEADS * V_HEAD).T

    bf = lambda a: a.astype(BF16)
    row = lambda a: a.reshape(1, -1)
    return {
        "wb": bf(win[:, :OFF_B]), "wc": bf(win[:, OFF_B:OFF_C]), "wh": bf(win[:, OFF_C:OFF_H]),
        "wqa": bf(win[:, OFF_H:OFF_QA]), "wkva": bf(win[:, OFF_QA:OFF_KVA]), "wkr": bf(wkr),
        "wg": bf(win[:, OFF_KR:]), "bgate": row(b_gate[l]),
        "qg": row(q_norm_g[l]), "wqb": bf(wqb), "kvg": row(kv_norm_g[l]),
        "wk": bf(wk), "wvt": bf(wvt),
        "convw": conv_w[l], "wco": bf(w_conv_out[l]), "wmo": bf(w_mla_o[l]), "wout": bf(w_out[l]),
        "lmg": row(ln_mix_g[l]), "lmb": row(ln_mix_b[l]),
        "wup": bf(w_up[l]), "wdown": bf(w_down[l]),
        "lfg": row(ln_ffn_g[l]), "lfb": row(ln_ffn_b[l]),
    }


def kernel(x, positions, ln_in_g, ln_in_b, w_in, b_gate, conv_w, w_conv_out, q_norm_g, w_q_b, kv_norm_g, w_kv_b, w_mla_o, w_out, ln_mix_g, ln_mix_b, w_up, w_down, ln_ffn_g, ln_ffn_b):
    batch, seq, d = x.shape
    depth = w_in.shape[0]
    t = batch * seq
    alpha = (2 * depth) ** 0.25
    tm = 512
    tq = 256
    kc = 512

    inv_freq = 1.0 / (ROPE_THETA ** (jnp.arange(0, QK_ROPE, 2, dtype=F32) / QK_ROPE))
    freq_row = jnp.concatenate([jnp.zeros((QK_NOPE,), F32)] + [inv_freq] * 4).reshape(1, HEAD_PAD)
    pos_b = jnp.broadcast_to(positions.reshape(t, 1), (t, HEAD_PAD))
    cos_t, sin_t = _rope_tables(pos_b, freq_row, tm)

    h = _entry_norm(x.reshape(t, d), ln_in_g.reshape(1, d), ln_in_b.reshape(1, d), tm)
    for l in range(depth):
        w = _prep_layer(l, w_in, b_gate, conv_w, w_conv_out, q_norm_g, w_q_b, kv_norm_g,
                        w_kv_b, w_mla_o, w_out, ln_mix_g, ln_mix_b, w_up, w_down,
                        ln_ffn_g, ln_ffn_b)
        u, q, k, vt = _proj(h, cos_t, sin_t, w, seq, tm)
        y = _attention(q, k, vt, seq, tq, kc)
        h = _mix(h, u, y, w, alpha, seq, tm)
        h = _mlp(h, w, alpha, tm)
    return h.reshape(batch, seq, d)
```

```python
import functools
import math

import jax
import jax.numpy as jnp
from jax import lax
from jax.experimental import pallas as pl
from jax.experimental.pallas import tpu as pltpu

D_MODEL = 1024
N_HEADS = 16
QK_NOPE = 64
QK_ROPE = 32
V_HEAD = 64
Q_LORA = 768
KV_LORA = 256
ROPE_THETA = 10000.0
D_FF = 4 * D_MODEL
LN_EPS = 1e-5
RMS_EPS = 1e-6
HEAD_PAD = 128
SUBLANES = 8
BF16_SUBLANES = 16

OFF_B = D_MODEL
OFF_C = 2 * D_MODEL
OFF_H = 3 * D_MODEL
OFF_QA = OFF_H + Q_LORA
OFF_KVA = OFF_QA + KV_LORA
OFF_KR = OFF_KVA + QK_ROPE

VMEM_LIMIT_BYTES = 56 * 1024 * 1024

BF16 = jnp.bfloat16
F32 = jnp.float32

_NT = (((1,), (1,)), ((), ()))


def _dot(a, b):
    return jnp.dot(a, b, preferred_element_type=F32)


def _layer_norm(z, g, b):
    mu = jnp.mean(z, axis=-1, keepdims=True)
    zc = z - mu
    var = jnp.mean(zc * zc, axis=-1, keepdims=True)
    return zc * lax.rsqrt(var + LN_EPS) * g + b


def _rms_norm(z, g):
    ms = jnp.mean(z * z, axis=-1, keepdims=True)
    return z * lax.rsqrt(ms + RMS_EPS) * g


def _const_spec(shape):
    nd = len(shape)
    return pl.BlockSpec(shape, lambda *_: (0,) * nd, pipeline_mode=pl.Buffered(1))


def _layer_spec(l, block, col_block=0):
    return pl.BlockSpec((None,) + tuple(block), lambda *_: (l, 0, col_block),
                        pipeline_mode=pl.Buffered(1))


def _params():
    return pltpu.CompilerParams(
        dimension_semantics=("arbitrary",), vmem_limit_bytes=VMEM_LIMIT_BYTES)


def _rope_table_kernel(pos_ref, freq_ref, cos_ref, sin_ref):
    ang = pos_ref[...].astype(F32) * freq_ref[...]
    cos_ref[...] = jnp.cos(ang)
    sin_ref[...] = jnp.sin(ang)


def _rope_tables(pos_b, freq_row, tm):
    t = pos_b.shape[0]
    row = pl.BlockSpec((tm, HEAD_PAD), lambda i: (i, 0))
    return pl.pallas_call(
        _rope_table_kernel,
        out_shape=(jax.ShapeDtypeStruct((t, HEAD_PAD), F32),) * 2,
        grid=(t // tm,),
        in_specs=[row, _const_spec((1, HEAD_PAD))],
        out_specs=(row, row),
        compiler_params=_params(),
        name="rope_tables",
    )(pos_b, freq_row)


def _ln_kernel(x_ref, g_ref, b_ref, o_ref):
    o_ref[...] = _layer_norm(x_ref[...], g_ref[...], b_ref[...])


def _entry_norm(x2, g, b, tm):
    t, d = x2.shape
    row = pl.BlockSpec((tm, d), lambda i: (i, 0))
    return pl.pallas_call(
        _ln_kernel,
        out_shape=jax.ShapeDtypeStruct((t, d), F32),
        grid=(t // tm,),
        in_specs=[row, _const_spec((1, d)), _const_spec((1, d))],
        out_specs=row,
        compiler_params=_params(),
        name="entry_norm",
    )(x2, g, b)


def _proj_kernel(q_scale, h_ref, cos_ref, sin_ref, wc_ref, wh_ref, wqa_ref, wkva_ref,
                 wkr_ref, qg_ref, wqb_ref, kvg_ref, wk_ref, wvt_ref,
                 u_ref, q_ref, k_ref, vt_ref):
    x = h_ref[...].astype(BF16)
    cos = cos_ref[...]
    sin = sin_ref[...]

    u_ref[...] = (_dot(x, wc_ref[...]) * _dot(x, wh_ref[...])).astype(BF16)

    lane = lax.broadcasted_iota(jnp.int32, cos.shape, 1)
    mq = jnp.where(lane < QK_NOPE, 1.0, jnp.where(lane < QK_NOPE + QK_ROPE, cos, sin)) * q_scale
    mq2 = jnp.concatenate([mq, mq], axis=1)
    qn = _rms_norm(_dot(x, wqa_ref[...]), qg_ref[...]).astype(BF16)
    for hp in range(N_HEADS // 2):
        sl = slice(hp * 2 * HEAD_PAD, (hp + 1) * 2 * HEAD_PAD)
        q_ref[:, sl] = (_dot(qn, wqb_ref[:, sl]) * mq2).astype(BF16)

    kr2 = _dot(x, wkr_ref[...])
    rk = kr2[:, :HEAD_PAD] * cos + kr2[:, HEAD_PAD:] * sin
    rk2 = jnp.concatenate([rk, rk], axis=1)
    kvn = _rms_norm(_dot(x, wkva_ref[...]), kvg_ref[...]).astype(BF16)
    for hp in range(N_HEADS // 2):
        sl = slice(hp * 2 * HEAD_PAD, (hp + 1) * 2 * HEAD_PAD)
        k_ref[:, sl] = (_dot(kvn, wk_ref[:, sl]) + rk2).astype(BF16)

    vt_ref[0] = lax.dot_general(wvt_ref[...], kvn, _NT,
                                preferred_element_type=F32).astype(BF16)


def _proj(l, h, cos_t, sin_t, w, seq, tm):
    t, d = h.shape
    nb = seq // tm
    hq = N_HEADS * HEAD_PAD
    hv = N_HEADS * V_HEAD
    row = lambda width: pl.BlockSpec((tm, width), lambda i: (i, 0))
    weights = ((w["w_main"], (d, d), OFF_B // d), (w["w_main"], (d, d), OFF_C // d),
               (w["w_main"], (d, Q_LORA), OFF_H // Q_LORA),
               (w["w_main"], (d, KV_LORA), OFF_QA // KV_LORA),
               (w["wkr"], (d, 2 * HEAD_PAD), 0), (w["qg"], (1, Q_LORA), 0),
               (w["wqb"], (Q_LORA, hq), 0), (w["kvg"], (1, KV_LORA), 0),
               (w["wk"], (KV_LORA, hq), 0), (w["wvt"], (hv, KV_LORA), 0))
    q_scale = (QK_NOPE + QK_ROPE) ** -0.5 * math.log2(math.e)
    return pl.pallas_call(
        functools.partial(_proj_kernel, q_scale),
        out_shape=(jax.ShapeDtypeStruct((t, d), BF16),
                   jax.ShapeDtypeStruct((t, hq), BF16),
                   jax.ShapeDtypeStruct((t, hq), BF16),
                   jax.ShapeDtypeStruct((t // seq, hv, seq), BF16)),
        grid=(t // tm,),
        in_specs=[row(d), row(HEAD_PAD), row(HEAD_PAD)]
        + [_layer_spec(l, blk, cb) for _, blk, cb in weights],
        out_specs=(row(d), row(hq), row(hq),
                   pl.BlockSpec((1, hv, tm), lambda i: (i // nb, 0, i % nb))),
        compiler_params=_params(),
        name="proj",
    )(h, cos_t, sin_t, *[a for a, _, _ in weights])


def _row_max8(x):
    while x.shape[0] > SUBLANES:
        half = x.shape[0] // 2
        x = jnp.maximum(x[:half], x[half:])
    return x


def _attn_kernel(kc, q_ref, k_ref, vt_ref, y_ref, s_buf, p_buf):
    seq = k_ref.shape[0]
    nc = seq // kc
    ones = jnp.ones((BF16_SUBLANES, kc), BF16)
    part_max, col_max, acc, outs = {}, {}, {}, {}

    def slot(h):
        return h % 2

    def score_step(h, c):
        q = q_ref[:, h * HEAD_PAD:(h + 1) * HEAD_PAD]
        k = k_ref[c * kc:(c + 1) * kc, h * HEAD_PAD:(h + 1) * HEAD_PAD]
        st = lax.dot_general(k, q, _NT, preferred_element_type=F32)
        s_buf[slot(h), c * kc:(c + 1) * kc, :] = st
        cm = _row_max8(st)
        part_max[h] = cm if c == 0 else jnp.maximum(part_max[h], cm)
        if c == nc - 1:
            col_max[h] = jnp.max(part_max.pop(h), axis=0, keepdims=True)

    def exp_step(h, c):
        s = s_buf[slot(h), c * kc:(c + 1) * kc, :]
        p_buf[slot(h), c * kc:(c + 1) * kc, :] = jnp.exp2(s - col_max[h]).astype(BF16)

    def value_step(h, c):
        v1 = jnp.concatenate(
            [vt_ref[0, h * V_HEAD:(h + 1) * V_HEAD, c * kc:(c + 1) * kc], ones], axis=0)
        part = _dot(v1, p_buf[slot(h), c * kc:(c + 1) * kc, :])
        acc[h] = part if c == 0 else acc[h] + part
        if c == nc - 1:
            ot = acc.pop(h)
            outs[h] = ot[:V_HEAD] / ot[V_HEAD:V_HEAD + 1]
            if h % 2 == 1:
                o2 = jnp.concatenate([outs.pop(h - 1), outs.pop(h)], axis=0)
                y_ref[:, (h - 1) * V_HEAD:(h + 1) * V_HEAD] = o2.T.astype(BF16)

    lag = 1
    total = N_HEADS * nc
    for g in range(total + 2 * (nc + lag)):
        for stage, step in enumerate((score_step, exp_step, value_step)):
            gs = g - stage * (nc + lag)
            if 0 <= gs < total:
                step(gs // nc, gs % nc)


def _attention(q, k, vt, seq, tq, kc):
    t, hq = q.shape
    nb = seq // tq
    hv = N_HEADS * V_HEAD
    return pl.pallas_call(
        functools.partial(_attn_kernel, kc),
        out_shape=jax.ShapeDtypeStruct((t, hv), BF16),
        grid=(t // tq,),
        in_specs=[pl.BlockSpec((tq, hq), lambda i: (i, 0)),
                  pl.BlockSpec((seq, hq), lambda i: (i // nb, 0)),
                  pl.BlockSpec((1, hv, seq), lambda i: (i // nb, 0, 0))],
        out_specs=pl.BlockSpec((tq, hv), lambda i: (i, 0)),
        scratch_shapes=[pltpu.VMEM((2, seq, tq), F32), pltpu.VMEM((2, seq, tq), BF16)],
        compiler_params=_params(),
        name="attention",
    )(q, k, vt)


def _mix_kernel(alpha, nb, h_ref, u_ref, up_ref, un_ref, y_ref, wb_ref, wg_ref, bg_ref,
                cw_ref, wco_ref, wmo_ref, wo_ref, lg_ref, lb_ref, o_ref):
    i = pl.program_id(0)
    hres = h_ref[...]
    x = hres.astype(BF16)
    tm = hres.shape[0]

    u = u_ref[...].astype(F32)
    halo = up_ref.shape[0]
    has_prev = (i % nb != 0).astype(F32)
    has_next = (i % nb != nb - 1).astype(F32)
    prev_row = up_ref[halo - 1:halo, :].astype(F32) * has_prev
    next_row = un_ref[0:1, :].astype(F32) * has_next
    r = lax.broadcasted_iota(jnp.int32, u.shape, 0)
    u_m1 = jnp.where(r == 0, prev_row, pltpu.roll(u, 1, axis=0))
    u_p1 = jnp.where(r == tm - 1, next_row, pltpu.roll(u, tm - 1, axis=0))
    cw = cw_ref[...]
    conv = cw[0:1, :] * u_m1 + cw[1:2, :] * u + cw[2:3, :] * u_p1

    yc = _dot((_dot(x, wb_ref[...]) * conv).astype(BF16), wco_ref[...])
    ym = _dot(y_ref[...], wmo_ref[...])
    g = jax.nn.sigmoid(_dot(x, wg_ref[...]) + bg_ref[...])
    merged = (g[:, :D_MODEL] * yc + g[:, D_MODEL:] * ym).astype(BF16)
    mix = _dot(merged, wo_ref[...])
    o_ref[...] = _layer_norm(alpha * hres + mix, lg_ref[...], lb_ref[...])


def _mix(l, h, u, y, w, alpha, seq, tm):
    t, d = h.shape
    nb = seq // tm
    halo = BF16_SUBLANES
    hb = tm // halo
    nhalo = t // halo
    row = lambda width: pl.BlockSpec((tm, width), lambda i: (i, 0))
    weights = ((w["w_main"], (d, d), 0), (w["wg"], (d, 2 * d), 0), (w["bgate"], (1, 2 * d), 0),
               (w["convw"], (3, d), 0), (w["wco"], (d, d), 0), (w["wmo"], (d, d), 0),
               (w["wout"], (d, d), 0), (w["lmg"], (1, d), 0), (w["lmb"], (1, d), 0))
    return pl.pallas_call(
        functools.partial(_mix_kernel, alpha, nb),
        out_shape=jax.ShapeDtypeStruct((t, d), F32),
        grid=(t // tm,),
        in_specs=[row(d), row(d),
                  pl.BlockSpec((halo, d), lambda i: (jnp.maximum(i * hb - 1, 0), 0)),
                  pl.BlockSpec((halo, d), lambda i: (jnp.minimum((i + 1) * hb, nhalo - 1), 0)),
                  row(d)] + [_layer_spec(l, blk, cb) for _, blk, cb in weights],
        out_specs=row(d),
        compiler_params=_params(),
        name="mix",
    )(h, u, u, u, y, *[a for a, _, _ in weights])


def _mlp_kernel(alpha, n_chunks, h_ref, wu_ref, wd_ref, lg_ref, lb_ref, o_ref):
    hres = h_ref[...]
    x = hres.astype(BF16)
    ck = D_FF // n_chunks
    acc = None
    for c in range(n_chunks):
        a = jnp.maximum(_dot(x, wu_ref[:, c * ck:(c + 1) * ck]), 0.0)
        part = _dot((a * a).astype(BF16), wd_ref[c * ck:(c + 1) * ck, :])
        acc = part if acc is None else acc + part
    o_ref[...] = _layer_norm(alpha * hres + acc, lg_ref[...], lb_ref[...])


def _mlp(l, h, w, alpha, tm):
    t, d = h.shape
    row = pl.BlockSpec((tm, d), lambda i: (i, 0))
    weights = ((w["wup"], (d, D_FF)), (w["wdown"], (D_FF, d)), (w["lfg"], (1, d)),
               (w["lfb"], (1, d)))
    return pl.pallas_call(
        functools.partial(_mlp_kernel, alpha, 4),
        out_shape=jax.ShapeDtypeStruct((t, d), F32),
        grid=(t // tm,),
        in_specs=[row] + [_layer_spec(l, blk) for _, blk in weights],
        out_specs=row,
        compiler_params=_params(),
        name="mlp",
    )(h, *[a for a, _ in weights])


def _rot_cols(w):
    half = QK_ROPE // 2
    return jnp.concatenate([-w[..., half:], w[..., :half]], axis=-1)


def _prep(w_in, b_gate, conv_w, w_conv_out, q_norm_g, w_q_b, kv_norm_g, w_kv_b,
          w_mla_o, w_out, ln_mix_g, ln_mix_b, w_up, w_down, ln_ffn_g, ln_ffn_b):
    depth, d = w_in.shape[0], D_MODEL
    wy = w_in[:, :, OFF_KVA:OFF_KR]
    z64 = jnp.zeros((depth, d, QK_NOPE), F32)
    wkr = jnp.concatenate([z64, wy, wy, z64, _rot_cols(wy), _rot_cols(wy)], axis=-1)

    wq = w_q_b.reshape(depth, Q_LORA, N_HEADS, QK_NOPE + QK_ROPE)
    wqb = jnp.concatenate([wq, _rot_cols(wq[..., QK_NOPE:])], axis=-1)
    wqb = wqb.reshape(depth, Q_LORA, N_HEADS * HEAD_PAD)

    wkv = w_kv_b.reshape(depth, KV_LORA, N_HEADS, QK_NOPE + V_HEAD)
    wk = jnp.concatenate([wkv[..., :QK_NOPE], jnp.zeros_like(wkv[..., QK_NOPE:])], axis=-1)
    wk = wk.reshape(depth, KV_LORA, N_HEADS * HEAD_PAD)
    wvt = jnp.swapaxes(wkv[..., QK_NOPE:].reshape(depth, KV_LORA, N_HEADS * V_HEAD), 1, 2)

    bf = lambda a: a.astype(BF16)
    row = lambda a: a.reshape(depth, 1, -1)
    return {
        "w_main": bf(w_in[:, :, :OFF_KVA]), "wkr": bf(wkr),
        "wg": bf(w_in[:, :, OFF_KR:]), "bgate": row(b_gate),
        "qg": row(q_norm_g), "wqb": bf(wqb), "kvg": row(kv_norm_g),
        "wk": bf(wk), "wvt": bf(wvt),
        "convw": conv_w, "wco": bf(w_conv_out), "wmo": bf(w_mla_o), "wout": bf(w_out),
        "lmg": row(ln_mix_g), "lmb": row(ln_mix_b),
        "wup": bf(w_up), "wdown": bf(w_down),
        "lfg": row(ln_ffn_g), "lfb": row(ln_ffn_b),
    }


def kernel(x, positions, ln_in_g, ln_in_b, w_in, b_gate, conv_w, w_conv_out, q_norm_g, w_q_b, kv_norm_g, w_kv_b, w_mla_o, w_out, ln_mix_g, ln_mix_b, w_up, w_down, ln_ffn_g, ln_ffn_b):
    batch, seq, d = x.shape
    depth = w_in.shape[0]
    t = batch * seq
    alpha = (2 * depth) ** 0.25
    tm = 512
    tq = 256
    kc = 512

    inv_freq = 1.0 / (ROPE_THETA ** (jnp.arange(0, QK_ROPE, 2, dtype=F32) / QK_ROPE))
    freq_row = jnp.concatenate([jnp.zeros((QK_NOPE,), F32)] + [inv_freq] * 4).reshape(1, HEAD_PAD)
    pos_b = jnp.broadcast_to(positions.reshape(t, 1), (t, HEAD_PAD))
    cos_t, sin_t = _rope_tables(pos_b, freq_row, tm)

    h = _entry_norm(x.reshape(t, d), ln_in_g.reshape(1, d), ln_in_b.reshape(1, d), tm)
    w = _prep(w_in, b_gate, conv_w, w_conv_out, q_norm_g, w_q_b, kv_norm_g, w_kv_b,
              w_mla_o, w_out, ln_mix_g, ln_mix_b, w_up, w_down, ln_ffn_g, ln_ffn_b)
    for l in range(depth):
        u, q, k, vt = _proj(l, h, cos_t, sin_t, w, seq, tm)
        y = _attention(q, k, vt, seq, tq, kc)
        h = _mix(l, h, u, y, w, alpha, seq, tm)
        h = _mlp(l, h, w, alpha, tm)
    return h.reshape(batch, seq, d)
```

```python
import functools
import math

import jax
import jax.numpy as jnp
from jax import lax
from jax.experimental import pallas as pl
from jax.experimental.pallas import tpu as pltpu

D_MODEL = 1024
N_HEADS = 16
QK_NOPE = 64
QK_ROPE = 32
V_HEAD = 64
Q_LORA = 768
KV_LORA = 256
ROPE_THETA = 10000.0
D_FF = 4 * D_MODEL
LN_EPS = 1e-5
RMS_EPS = 1e-6
HEAD_PAD = 128
SUBLANES = 8
BF16_SUBLANES = 16

OFF_B = D_MODEL
OFF_C = 2 * D_MODEL
OFF_H = 3 * D_MODEL
OFF_QA = OFF_H + Q_LORA
OFF_KVA = OFF_QA + KV_LORA
OFF_KR = OFF_KVA + QK_ROPE

VMEM_LIMIT_BYTES = 56 * 1024 * 1024

BF16 = jnp.bfloat16
F32 = jnp.float32

_NT = (((1,), (1,)), ((), ()))


def _dot(a, b):
    return jnp.dot(a, b, preferred_element_type=F32)


def _dot_t(a, bt):
    return lax.dot_general(a, bt, _NT, preferred_element_type=F32)


def _layer_norm(z, g, b):
    mu = jnp.mean(z, axis=-1, keepdims=True)
    zc = z - mu
    var = jnp.mean(zc * zc, axis=-1, keepdims=True)
    return zc * lax.rsqrt(var + LN_EPS) * g + b


def _rms_norm(z, g):
    ms = jnp.mean(z * z, axis=-1, keepdims=True)
    return z * lax.rsqrt(ms + RMS_EPS) * g


def _const_spec(shape):
    nd = len(shape)
    return pl.BlockSpec(shape, lambda *_: (0,) * nd, pipeline_mode=pl.Buffered(1))


def _layer_spec(l, block, row_block=0):
    return pl.BlockSpec((None,) + tuple(block), lambda *_: (l, row_block, 0),
                        pipeline_mode=pl.Buffered(1))


def _params():
    return pltpu.CompilerParams(
        dimension_semantics=("arbitrary",), vmem_limit_bytes=VMEM_LIMIT_BYTES)


def _rope_table_kernel(pos_ref, freq_ref, cos_ref, sin_ref):
    ang = pos_ref[...].astype(F32) * freq_ref[...]
    cos_ref[...] = jnp.cos(ang)
    sin_ref[...] = jnp.sin(ang)


def _rope_tables(pos_b, freq_row, tm):
    t = pos_b.shape[0]
    row = pl.BlockSpec((tm, HEAD_PAD), lambda i: (i, 0))
    return pl.pallas_call(
        _rope_table_kernel,
        out_shape=(jax.ShapeDtypeStruct((t, HEAD_PAD), F32),) * 2,
        grid=(t // tm,),
        in_specs=[row, _const_spec((1, HEAD_PAD))],
        out_specs=(row, row),
        compiler_params=_params(),
        name="rope_tables",
    )(pos_b, freq_row)


def _ln_kernel(x_ref, g_ref, b_ref, o_ref):
    o_ref[...] = _layer_norm(x_ref[...], g_ref[...], b_ref[...])


def _entry_norm(x2, g, b, tm):
    t, d = x2.shape
    row = pl.BlockSpec((tm, d), lambda i: (i, 0))
    return pl.pallas_call(
        _ln_kernel,
        out_shape=jax.ShapeDtypeStruct((t, d), F32),
        grid=(t // tm,),
        in_specs=[row, _const_spec((1, d)), _const_spec((1, d))],
        out_specs=row,
        compiler_params=_params(),
        name="entry_norm",
    )(x2, g, b)


def _proj_kernel(q_scale, h_ref, cos_ref, sin_ref, wct_ref, wht_ref, wqat_ref, wkvat_ref,
                 wkrt_ref, qg_ref, wqb_ref, kvg_ref, wk_ref, wvt_ref,
                 u_ref, q_ref, k_ref, vt_ref):
    x = h_ref[...].astype(BF16)
    cos = cos_ref[...]
    sin = sin_ref[...]

    u_ref[...] = (_dot_t(x, wct_ref[...]) * _dot_t(x, wht_ref[...])).astype(BF16)

    lane = lax.broadcasted_iota(jnp.int32, cos.shape, 1)
    mq = jnp.where(lane < QK_NOPE, 1.0, jnp.where(lane < QK_NOPE + QK_ROPE, cos, sin)) * q_scale
    mq2 = jnp.concatenate([mq, mq], axis=1)
    qn = _rms_norm(_dot_t(x, wqat_ref[...]), qg_ref[...]).astype(BF16)
    for hp in range(N_HEADS // 2):
        sl = slice(hp * 2 * HEAD_PAD, (hp + 1) * 2 * HEAD_PAD)
        q_ref[:, sl] = (_dot(qn, wqb_ref[:, sl]) * mq2).astype(BF16)

    kr2 = _dot_t(x, wkrt_ref[...])
    rk = kr2[:, :HEAD_PAD] * cos + kr2[:, HEAD_PAD:] * sin
    rk2 = jnp.concatenate([rk, rk], axis=1)
    kvn = _rms_norm(_dot_t(x, wkvat_ref[...]), kvg_ref[...]).astype(BF16)
    for hp in range(N_HEADS // 2):
        sl = slice(hp * 2 * HEAD_PAD, (hp + 1) * 2 * HEAD_PAD)
        k_ref[:, sl] = (_dot(kvn, wk_ref[:, sl]) + rk2).astype(BF16)

    vt_ref[0] = lax.dot_general(wvt_ref[...], kvn, _NT,
                                preferred_element_type=F32).astype(BF16)


def _proj(l, h, cos_t, sin_t, w, seq, tm):
    t, d = h.shape
    nb = seq // tm
    hq = N_HEADS * HEAD_PAD
    hv = N_HEADS * V_HEAD
    row = lambda width: pl.BlockSpec((tm, width), lambda i: (i, 0))
    weights = ((w["w_main_t"], (d, d), OFF_B // d), (w["w_main_t"], (d, d), OFF_C // d),
               (w["w_main_t"], (Q_LORA, d), OFF_H // Q_LORA),
               (w["w_main_t"], (KV_LORA, d), OFF_QA // KV_LORA),
               (w["wkr_t"], (2 * HEAD_PAD, d), 0), (w["qg"], (1, Q_LORA), 0),
               (w["wqb"], (Q_LORA, hq), 0), (w["kvg"], (1, KV_LORA), 0),
               (w["wk"], (KV_LORA, hq), 0), (w["wvt"], (hv, KV_LORA), 0))
    q_scale = (QK_NOPE + QK_ROPE) ** -0.5 * math.log2(math.e)
    return pl.pallas_call(
        functools.partial(_proj_kernel, q_scale),
        out_shape=(jax.ShapeDtypeStruct((t, d), BF16),
                   jax.ShapeDtypeStruct((t, hq), BF16),
                   jax.ShapeDtypeStruct((t, hq), BF16),
                   jax.ShapeDtypeStruct((t // seq, hv, seq), BF16)),
        grid=(t // tm,),
        in_specs=[row(d), row(HEAD_PAD), row(HEAD_PAD)]
        + [_layer_spec(l, blk, cb) for _, blk, cb in weights],
        out_specs=(row(d), row(hq), row(hq),
                   pl.BlockSpec((1, hv, tm), lambda i: (i // nb, 0, i % nb))),
        compiler_params=_params(),
        name="proj",
    )(h, cos_t, sin_t, *[a for a, _, _ in weights])


def _row_max8(x):
    while x.shape[0] > SUBLANES:
        half = x.shape[0] // 2
        x = jnp.maximum(x[:half], x[half:])
    return x


def _attn_kernel(kc, lag, q_ref, k_ref, vt_ref, y_ref, s_buf, p_buf):
    seq = k_ref.shape[0]
    nc = seq // kc
    ring = s_buf.shape[0]
    ones = jnp.ones((BF16_SUBLANES, kc), BF16)
    chunk_max, parts, outs = {}, {}, {}

    def score_step(g, h, c):
        q = q_ref[:, h * HEAD_PAD:(h + 1) * HEAD_PAD]
        k = k_ref[c * kc:(c + 1) * kc, h * HEAD_PAD:(h + 1) * HEAD_PAD]
        st = lax.dot_general(k, q, _NT, preferred_element_type=F32)
        s_buf[g % ring] = st
        chunk_max[g] = jnp.max(_row_max8(st), axis=0, keepdims=True)

    def exp_step(g, h, c):
        p_buf[g % ring] = jnp.exp2(s_buf[g % ring] - chunk_max[g]).astype(BF16)

    def value_step(g, h, c):
        v1 = jnp.concatenate(
            [vt_ref[0, h * V_HEAD:(h + 1) * V_HEAD, c * kc:(c + 1) * kc], ones], axis=0)
        parts[g] = _dot(v1, p_buf[g % ring])
        if c == nc - 1:
            gs = range(g - nc + 1, g + 1)
            m = functools.reduce(jnp.maximum, [chunk_max[i] for i in gs])
            ot = sum(parts.pop(i) * jnp.exp2(chunk_max.pop(i) - m) for i in gs)
            outs[h] = ot[:V_HEAD] / ot[V_HEAD:V_HEAD + 1]
            if h % 2 == 1:
                o2 = jnp.concatenate([outs.pop(h - 1), outs.pop(h)], axis=0)
                y_ref[:, (h - 1) * V_HEAD:(h + 1) * V_HEAD] = o2.T.astype(BF16)

    total = N_HEADS * nc
    for step in range(total + 2 * lag):
        for stage, fn in enumerate((score_step, exp_step, value_step)):
            g = step - stage * lag
            if 0 <= g < total:
                fn(g, g // nc, g % nc)


def _attention(q, k, vt, seq, tq, kc, lag):
    t, hq = q.shape
    nb = seq // tq
    hv = N_HEADS * V_HEAD
    ring = lag + 1
    return pl.pallas_call(
        functools.partial(_attn_kernel, kc, lag),
        out_shape=jax.ShapeDtypeStruct((t, hv), BF16),
        grid=(t // tq,),
        in_specs=[pl.BlockSpec((tq, hq), lambda i: (i, 0)),
                  pl.BlockSpec((seq, hq), lambda i: (i // nb, 0)),
                  pl.BlockSpec((1, hv, seq), lambda i: (i // nb, 0, 0))],
        out_specs=pl.BlockSpec((tq, hv), lambda i: (i, 0)),
        scratch_shapes=[pltpu.VMEM((ring, kc, tq), F32), pltpu.VMEM((ring, kc, tq), BF16)],
        compiler_params=_params(),
        name="attention",
    )(q, k, vt)


def _mix_kernel(alpha, nb, h_ref, u_ref, up_ref, un_ref, y_ref, wbt_ref, wgt_ref, bg_ref,
                cw_ref, wco_ref, wmo_ref, wo_ref, lg_ref, lb_ref, o_ref):
    i = pl.program_id(0)
    hres = h_ref[...]
    x = hres.astype(BF16)
    tm = hres.shape[0]

    u = u_ref[...].astype(F32)
    halo = up_ref.shape[0]
    has_prev = (i % nb != 0).astype(F32)
    has_next = (i % nb != nb - 1).astype(F32)
    prev_row = up_ref[halo - 1:halo, :].astype(F32) * has_prev
    next_row = un_ref[0:1, :].astype(F32) * has_next
    r = lax.broadcasted_iota(jnp.int32, u.shape, 0)
    u_m1 = jnp.where(r == 0, prev_row, pltpu.roll(u, 1, axis=0))
    u_p1 = jnp.where(r == tm - 1, next_row, pltpu.roll(u, tm - 1, axis=0))
    cw = cw_ref[...]
    conv = cw[0:1, :] * u_m1 + cw[1:2, :] * u + cw[2:3, :] * u_p1

    yc = _dot((_dot_t(x, wbt_ref[...]) * conv).astype(BF16), wco_ref[...])
    ym = _dot(y_ref[...], wmo_ref[...])
    g = jax.nn.sigmoid(_dot_t(x, wgt_ref[...]) + bg_ref[...])
    merged = (g[:, :D_MODEL] * yc + g[:, D_MODEL:] * ym).astype(BF16)
    mix = _dot(merged, wo_ref[...])
    o_ref[...] = _layer_norm(alpha * hres + mix, lg_ref[...], lb_ref[...])


def _mix(l, h, u, y, w, alpha, seq, tm):
    t, d = h.shape
    nb = seq // tm
    halo = BF16_SUBLANES
    hb = tm // halo
    nhalo = t // halo
    row = lambda width: pl.BlockSpec((tm, width), lambda i: (i, 0))
    weights = ((w["w_main_t"], (d, d), 0), (w["wg_t"], (2 * d, d), 0), (w["bgate"], (1, 2 * d), 0),
               (w["convw"], (3, d), 0), (w["wco"], (d, d), 0), (w["wmo"], (d, d), 0),
               (w["wout"], (d, d), 0), (w["lmg"], (1, d), 0), (w["lmb"], (1, d), 0))
    return pl.pallas_call(
        functools.partial(_mix_kernel, alpha, nb),
        out_shape=jax.ShapeDtypeStruct((t, d), F32),
        grid=(t // tm,),
        in_specs=[row(d), row(d),
                  pl.BlockSpec((halo, d), lambda i: (jnp.maximum(i * hb - 1, 0), 0)),
                  pl.BlockSpec((halo, d), lambda i: (jnp.minimum((i + 1) * hb, nhalo - 1), 0)),
                  row(d)] + [_layer_spec(l, blk, cb) for _, blk, cb in weights],
        out_specs=row(d),
        compiler_params=_params(),
        name="mix",
    )(h, u, u, u, y, *[a for a, _, _ in weights])


def _mlp_kernel(alpha, n_chunks, h_ref, wu_ref, wd_ref, lg_ref, lb_ref, o_ref):
    hres = h_ref[...]
    x = hres.astype(BF16)
    ck = D_FF // n_chunks
    acc = None
    for c in range(n_chunks):
        a = jnp.maximum(_dot(x, wu_ref[:, c * ck:(c + 1) * ck]), 0.0)
        part = _dot((a * a).astype(BF16), wd_ref[c * ck:(c + 1) * ck, :])
        acc = part if acc is None else acc + part
    o_ref[...] = _layer_norm(alpha * hres + acc, lg_ref[...], lb_ref[...])


def _mlp(l, h, w, alpha, tm):
    t, d = h.shape
    row = pl.BlockSpec((tm, d), lambda i: (i, 0))
    weights = ((w["wup"], (d, D_FF)), (w["wdown"], (D_FF, d)), (w["lfg"], (1, d)),
               (w["lfb"], (1, d)))
    return pl.pallas_call(
        functools.partial(_mlp_kernel, alpha, 4),
        out_shape=jax.ShapeDtypeStruct((t, d), F32),
        grid=(t // tm,),
        in_specs=[row] + [_layer_spec(l, blk) for _, blk in weights],
        out_specs=row,
        compiler_params=_params(),
        name="mlp",
    )(h, *[a for a, _ in weights])


def _rot_cols(w):
    half = QK_ROPE // 2
    return jnp.concatenate([-w[..., half:], w[..., :half]], axis=-1)


def _prep(w_in, b_gate, conv_w, w_conv_out, q_norm_g, w_q_b, kv_norm_g, w_kv_b,
          w_mla_o, w_out, ln_mix_g, ln_mix_b, w_up, w_down, ln_ffn_g, ln_ffn_b):
    depth, d = w_in.shape[0], D_MODEL
    w_in_t = jnp.swapaxes(w_in, 1, 2)
    wy_t = w_in_t[:, OFF_KVA:OFF_KR, :]
    wyr_t = jnp.concatenate([-wy_t[:, QK_ROPE // 2:], wy_t[:, :QK_ROPE // 2]], axis=1)
    z64 = jnp.zeros((depth, QK_NOPE, d), F32)
    wkr_t = jnp.concatenate([z64, wy_t, wy_t, z64, wyr_t, wyr_t], axis=1)

    wq = w_q_b.reshape(depth, Q_LORA, N_HEADS, QK_NOPE + QK_ROPE)
    wqb = jnp.concatenate([wq, _rot_cols(wq[..., QK_NOPE:])], axis=-1)
    wqb = wqb.reshape(depth, Q_LORA, N_HEADS * HEAD_PAD)

    wkv = w_kv_b.reshape(depth, KV_LORA, N_HEADS, QK_NOPE + V_HEAD)
    wk = jnp.concatenate([wkv[..., :QK_NOPE], jnp.zeros_like(wkv[..., QK_NOPE:])], axis=-1)
    wk = wk.reshape(depth, KV_LORA, N_HEADS * HEAD_PAD)
    wvt = jnp.swapaxes(wkv[..., QK_NOPE:].reshape(depth, KV_LORA, N_HEADS * V_HEAD), 1, 2)

    bf = lambda a: a.astype(BF16)
    row = lambda a: a.reshape(depth, 1, -1)
    return {
        "w_main_t": bf(w_in_t), "wkr_t": bf(wkr_t),
        "wg_t": bf(w_in_t[:, OFF_KR:, :]), "bgate": row(b_gate),
        "qg": row(q_norm_g), "wqb": bf(wqb), "kvg": row(kv_norm_g),
        "wk": bf(wk), "wvt": bf(wvt),
        "convw": conv_w, "wco": bf(w_conv_out), "wmo": bf(w_mla_o), "wout": bf(w_out),
        "lmg": row(ln_mix_g), "lmb": row(ln_mix_b),
        "wup": bf(w_up), "wdown": bf(w_down),
        "lfg": row(ln_ffn_g), "lfb": row(ln_ffn_b),
    }


def kernel(x, positions, ln_in_g, ln_in_b, w_in, b_gate, conv_w, w_conv_out, q_norm_g, w_q_b, kv_norm_g, w_kv_b, w_mla_o, w_out, ln_mix_g, ln_mix_b, w_up, w_down, ln_ffn_g, ln_ffn_b):
    batch, seq, d = x.shape
    depth = w_in.shape[0]
    t = batch * seq
    alpha = (2 * depth) ** 0.25
    tm = 512
    tq = 256
    kc = 256
    lag = 3

    inv_freq = 1.0 / (ROPE_THETA ** (jnp.arange(0, QK_ROPE, 2, dtype=F32) / QK_ROPE))
    freq_row = jnp.concatenate([jnp.zeros((QK_NOPE,), F32)] + [inv_freq] * 4).reshape(1, HEAD_PAD)
    pos_b = jnp.broadcast_to(positions.reshape(t, 1), (t, HEAD_PAD))
    cos_t, sin_t = _rope_tables(pos_b, freq_row, tm)

    h = _entry_norm(x.reshape(t, d), ln_in_g.reshape(1, d), ln_in_b.reshape(1, d), tm)
    w = _prep(w_in, b_gate, conv_w, w_conv_out, q_norm_g, w_q_b, kv_norm_g, w_kv_b,
              w_mla_o, w_out, ln_mix_g, ln_mix_b, w_up, w_down, ln_ffn_g, ln_ffn_b)
    for l in range(depth):
        u, q, k, vt = _proj(l, h, cos_t, sin_t, w, seq, tm)
        y = _attention(q, k, vt, seq, tq, kc, lag)
        h = _mix(l, h, u, y, w, alpha, seq, tm)
        h = _mlp(l, h, w, alpha, 2 * tm)
    return h.reshape(batch, seq, d)
```

```python
import functools
import math

import jax
import jax.numpy as jnp
from jax import lax
from jax.experimental import pallas as pl
from jax.experimental.pallas import tpu as pltpu

D_MODEL = 1024
N_HEADS = 16
QK_NOPE = 64
QK_ROPE = 32
V_HEAD = 64
Q_LORA = 768
KV_LORA = 256
ROPE_THETA = 10000.0
D_FF = 4 * D_MODEL
LN_EPS = 1e-5
RMS_EPS = 1e-6
HEAD_PAD = 128
SUBLANES = 8
BF16_SUBLANES = 16

OFF_B = D_MODEL
OFF_C = 2 * D_MODEL
OFF_H = 3 * D_MODEL
OFF_QA = OFF_H + Q_LORA
OFF_KVA = OFF_QA + KV_LORA
OFF_KR = OFF_KVA + QK_ROPE

VMEM_LIMIT_BYTES = 56 * 1024 * 1024

BF16 = jnp.bfloat16
F32 = jnp.float32

_NT = (((1,), (1,)), ((), ()))


def _dot(a, b):
    return jnp.dot(a, b, preferred_element_type=F32)


def _dot_t(a, bt):
    return lax.dot_general(a, bt, _NT, preferred_element_type=F32)


def _layer_norm(z, g, b):
    mu = jnp.mean(z, axis=-1, keepdims=True)
    zc = z - mu
    var = jnp.mean(zc * zc, axis=-1, keepdims=True)
    return zc * lax.rsqrt(var + LN_EPS) * g + b


def _rms_norm(z, g):
    ms = jnp.mean(z * z, axis=-1, keepdims=True)
    return z * lax.rsqrt(ms + RMS_EPS) * g


def _const_spec(shape):
    nd = len(shape)
    return pl.BlockSpec(shape, lambda *_: (0,) * nd, pipeline_mode=pl.Buffered(1))


def _layer_spec(l, block, row_block=0):
    return pl.BlockSpec((None,) + tuple(block), lambda *_: (l, row_block, 0),
                        pipeline_mode=pl.Buffered(1))


def _params():
    return pltpu.CompilerParams(
        dimension_semantics=("arbitrary",), vmem_limit_bytes=VMEM_LIMIT_BYTES)


def _rope_table_kernel(pos_ref, freq_ref, cos_ref, sin_ref):
    ang = pos_ref[...].astype(F32) * freq_ref[...]
    cos_ref[...] = jnp.cos(ang)
    sin_ref[...] = jnp.sin(ang)


def _rope_tables(pos_b, freq_row, tm):
    t = pos_b.shape[0]
    row = pl.BlockSpec((tm, HEAD_PAD), lambda i: (i, 0))
    return pl.pallas_call(
        _rope_table_kernel,
        out_shape=(jax.ShapeDtypeStruct((t, HEAD_PAD), F32),) * 2,
        grid=(t // tm,),
        in_specs=[row, _const_spec((1, HEAD_PAD))],
        out_specs=(row, row),
        compiler_params=_params(),
        name="rope_tables",
    )(pos_b, freq_row)


def _proj_kernel(q_scale, sub, entry_norm, h_ref, *refs):
    if entry_norm:
        lng_ref, lnb_ref, *refs, h0_ref = refs
    (cos_ref, sin_ref, wct_ref, wht_ref, wqat_ref, wkvat_ref, wkrt_ref, qg_ref, wqb_ref,
     kvg_ref, wk_ref, wvt_ref, u_ref, q_ref, k_ref, vt_ref) = refs
    for j in range(h_ref.shape[0] // sub):
        rows = slice(j * sub, (j + 1) * sub)
        hres = h_ref[rows, :]
        if entry_norm:
            hres = _layer_norm(hres, lng_ref[...], lnb_ref[...])
            h0_ref[rows, :] = hres
        x = hres.astype(BF16)
        cos = cos_ref[rows, :]
        sin = sin_ref[rows, :]

        u_ref[rows, :] = (_dot_t(x, wct_ref[...]) * _dot_t(x, wht_ref[...])).astype(BF16)

        lane = lax.broadcasted_iota(jnp.int32, cos.shape, 1)
        mq = jnp.where(lane < QK_NOPE, 1.0,
                       jnp.where(lane < QK_NOPE + QK_ROPE, cos, sin)) * q_scale
        mq2 = jnp.concatenate([mq, mq], axis=1)
        qn = _rms_norm(_dot_t(x, wqat_ref[...]), qg_ref[...]).astype(BF16)
        for hp in range(N_HEADS // 2):
            sl = slice(hp * 2 * HEAD_PAD, (hp + 1) * 2 * HEAD_PAD)
            q_ref[rows, sl] = (_dot(qn, wqb_ref[:, sl]) * mq2).astype(BF16)

        kr2 = _dot_t(x, wkrt_ref[...])
        rk = kr2[:, :HEAD_PAD] * cos + kr2[:, HEAD_PAD:] * sin
        rk2 = jnp.concatenate([rk, rk], axis=1)
        kvn = _rms_norm(_dot_t(x, wkvat_ref[...]), kvg_ref[...]).astype(BF16)
        for hp in range(N_HEADS // 2):
            sl = slice(hp * 2 * HEAD_PAD, (hp + 1) * 2 * HEAD_PAD)
            k_ref[rows, sl] = (_dot(kvn, wk_ref[:, sl]) + rk2).astype(BF16)

        vt_ref[0, :, rows] = lax.dot_general(wvt_ref[...], kvn, _NT,
                                             preferred_element_type=F32).astype(BF16)


def _proj(l, h, cos_t, sin_t, w, seq, tm, sub, entry_ln=None):
    t, d = h.shape
    nb = seq // tm
    hq = N_HEADS * HEAD_PAD
    hv = N_HEADS * V_HEAD
    row = lambda width: pl.BlockSpec((tm, width), lambda i: (i, 0))
    ln_args = () if entry_ln is None else tuple(entry_ln)
    ln_specs = [_const_spec(a.shape) for a in ln_args]
    extra_shape = () if entry_ln is None else (jax.ShapeDtypeStruct((t, d), F32),)
    extra_spec = () if entry_ln is None else (row(d),)
    weights = ((w["w_main_t"], (d, d), OFF_B // d), (w["w_main_t"], (d, d), OFF_C // d),
               (w["w_main_t"], (Q_LORA, d), OFF_H // Q_LORA),
               (w["w_main_t"], (KV_LORA, d), OFF_QA // KV_LORA),
               (w["wkr_t"], (2 * HEAD_PAD, d), 0), (w["qg"], (1, Q_LORA), 0),
               (w["wqb"], (Q_LORA, hq), 0), (w["kvg"], (1, KV_LORA), 0),
               (w["wk"], (KV_LORA, hq), 0), (w["wvt"], (hv, KV_LORA), 0))
    q_scale = (QK_NOPE + QK_ROPE) ** -0.5 * math.log2(math.e)
    return pl.pallas_call(
        functools.partial(_proj_kernel, q_scale, sub, entry_ln is not None),
        out_shape=(jax.ShapeDtypeStruct((t, d), BF16),
                   jax.ShapeDtypeStruct((t, hq), BF16),
                   jax.ShapeDtypeStruct((t, hq), BF16),
                   jax.ShapeDtypeStruct((t // seq, hv, seq), BF16)) + extra_shape,
        grid=(t // tm,),
        in_specs=[row(d)] + ln_specs + [row(HEAD_PAD), row(HEAD_PAD)]
        + [_layer_spec(l, blk, cb) for _, blk, cb in weights],
        out_specs=(row(d), row(hq), row(hq),
                   pl.BlockSpec((1, hv, tm), lambda i: (i // nb, 0, i % nb))) + extra_spec,
        compiler_params=_params(),
        name="proj",
    )(h, *ln_args, cos_t, sin_t, *[a for a, _, _ in weights])


def _row_max8(x):
    while x.shape[0] > SUBLANES:
        half = x.shape[0] // 2
        x = jnp.maximum(x[:half], x[half:])
    return x


def _attn_tile(kc, lag, row0, q_ref, k_ref, vt_ref, y_ref, s_buf, p_buf):
    seq = k_ref.shape[0]
    tq = s_buf.shape[2]
    nc = seq // kc
    ring = s_buf.shape[0]
    rows = pl.ds(row0, tq)
    ones = jnp.ones((BF16_SUBLANES, kc), BF16)
    chunk_max, parts, outs = {}, {}, {}

    def score_step(g, h, c):
        q = q_ref[rows, h * HEAD_PAD:(h + 1) * HEAD_PAD]
        k = k_ref[c * kc:(c + 1) * kc, h * HEAD_PAD:(h + 1) * HEAD_PAD]
        st = lax.dot_general(k, q, _NT, preferred_element_type=F32)
        s_buf[g % ring] = st
        chunk_max[g] = jnp.max(_row_max8(st), axis=0, keepdims=True)

    def exp_step(g, h, c):
        p_buf[g % ring] = jnp.exp2(s_buf[g % ring] - chunk_max[g]).astype(BF16)

    def value_step(g, h, c):
        v1 = jnp.concatenate(
            [vt_ref[0, h * V_HEAD:(h + 1) * V_HEAD, c * kc:(c + 1) * kc], ones], axis=0)
        parts[g] = _dot(v1, p_buf[g % ring])
        if c == nc - 1:
            gs = range(g - nc + 1, g + 1)
            m = functools.reduce(jnp.maximum, [chunk_max[i] for i in gs])
            ot = sum(parts.pop(i) * jnp.exp2(chunk_max.pop(i) - m) for i in gs)
            outs[h] = ot[:V_HEAD] / ot[V_HEAD:V_HEAD + 1]
            if h % 2 == 1:
                o2 = jnp.concatenate([outs.pop(h - 1), outs.pop(h)], axis=0)
                y_ref[rows, (h - 1) * V_HEAD:(h + 1) * V_HEAD] = o2.T.astype(BF16)

    total = N_HEADS * nc
    for step in range(total + 2 * lag):
        for stage, fn in enumerate((score_step, exp_step, value_step)):
            g = step - stage * lag
            if 0 <= g < total:
                fn(g, g // nc, g % nc)


def _attn_kernel(kc, lag, q_ref, k_ref, vt_ref, y_ref, s_buf, p_buf):
    tq = s_buf.shape[2]

    def tile(j, carry):
        _attn_tile(kc, lag, pl.multiple_of(j * tq, tq), q_ref, k_ref, vt_ref, y_ref,
                   s_buf, p_buf)
        return carry

    lax.fori_loop(0, q_ref.shape[0] // tq, tile, 0)


def _attention(q, k, vt, seq, tq, n_sub, kc, lag):
    t, hq = q.shape
    tb = tq * n_sub
    nb = seq // tb
    hv = N_HEADS * V_HEAD
    ring = lag + 1
    return pl.pallas_call(
        functools.partial(_attn_kernel, kc, lag),
        out_shape=jax.ShapeDtypeStruct((t, hv), BF16),
        grid=(t // tb,),
        in_specs=[pl.BlockSpec((tb, hq), lambda i: (i, 0)),
                  pl.BlockSpec((seq, hq), lambda i: (i // nb, 0)),
                  pl.BlockSpec((1, hv, seq), lambda i: (i // nb, 0, 0))],
        out_specs=pl.BlockSpec((tb, hv), lambda i: (i, 0)),
        scratch_shapes=[pltpu.VMEM((ring, kc, tq), F32), pltpu.VMEM((ring, kc, tq), BF16)],
        compiler_params=_params(),
        name="attention",
    )(q, k, vt)


def _mix_kernel(alpha, nb, sub, h_ref, u_ref, up_ref, un_ref, y_ref, wbt_ref, wgt_ref, bg_ref,
                cw_ref, wco_ref, wmo_ref, wo_ref, lg_ref, lb_ref, o_ref):
    i = pl.program_id(0)
    tm = h_ref.shape[0]
    halo = up_ref.shape[0]
    has_prev = (i % nb != 0).astype(F32)
    has_next = (i % nb != nb - 1).astype(F32)
    cw = cw_ref[...]
    for j in range(tm // sub):
        lo, hi = j * sub, (j + 1) * sub
        hres = h_ref[lo:hi, :]
        x = hres.astype(BF16)

        u = u_ref[lo:hi, :].astype(F32)
        if j == 0:
            prev_row = up_ref[halo - 1:halo, :].astype(F32) * has_prev
        else:
            prev_row = u_ref[lo - 1:lo, :].astype(F32)
        if hi == tm:
            next_row = un_ref[0:1, :].astype(F32) * has_next
        else:
            next_row = u_ref[hi:hi + 1, :].astype(F32)
        r = lax.broadcasted_iota(jnp.int32, u.shape, 0)
        u_m1 = jnp.where(r == 0, prev_row, pltpu.roll(u, 1, axis=0))
        u_p1 = jnp.where(r == sub - 1, next_row, pltpu.roll(u, sub - 1, axis=0))
        conv = cw[0:1, :] * u_m1 + cw[1:2, :] * u + cw[2:3, :] * u_p1

        yc = _dot((_dot_t(x, wbt_ref[...]) * conv).astype(BF16), wco_ref[...])
        ym = _dot(y_ref[lo:hi, :], wmo_ref[...])
        g = jax.nn.sigmoid(_dot_t(x, wgt_ref[...]) + bg_ref[...])
        merged = (g[:, :D_MODEL] * yc + g[:, D_MODEL:] * ym).astype(BF16)
        mix = _dot(merged, wo_ref[...])
        o_ref[lo:hi, :] = _layer_norm(alpha * hres + mix, lg_ref[...], lb_ref[...])


def _mix(l, h, u, y, w, alpha, seq, tm, sub):
    t, d = h.shape
    nb = seq // tm
    halo = BF16_SUBLANES
    hb = tm // halo
    nhalo = t // halo
    row = lambda width: pl.BlockSpec((tm, width), lambda i: (i, 0))
    weights = ((w["w_main_t"], (d, d), 0), (w["wg_t"], (2 * d, d), 0), (w["bgate"], (1, 2 * d), 0),
               (w["convw"], (3, d), 0), (w["wco"], (d, d), 0), (w["wmo"], (d, d), 0),
               (w["wout"], (d, d), 0), (w["lmg"], (1, d), 0), (w["lmb"], (1, d), 0))
    return pl.pallas_call(
        functools.partial(_mix_kernel, alpha, nb, sub),
        out_shape=jax.ShapeDtypeStruct((t, d), F32),
        grid=(t // tm,),
        in_specs=[row(d), row(d),
                  pl.BlockSpec((halo, d), lambda i: (jnp.maximum(i * hb - 1, 0), 0)),
                  pl.BlockSpec((halo, d), lambda i: (jnp.minimum((i + 1) * hb, nhalo - 1), 0)),
                  row(d)] + [_layer_spec(l, blk, cb) for _, blk, cb in weights],
        out_specs=row(d),
        compiler_params=_params(),
        name="mix",
    )(h, u, u, u, y, *[a for a, _, _ in weights])


def _mlp_kernel(alpha, n_chunks, h_ref, wu_ref, wd_ref, lg_ref, lb_ref, o_ref):
    hres = h_ref[...]
    x = hres.astype(BF16)
    ck = D_FF // n_chunks
    acc = None
    for c in range(n_chunks):
        a = jnp.maximum(_dot(x, wu_ref[:, c * ck:(c + 1) * ck]), 0.0)
        part = _dot((a * a).astype(BF16), wd_ref[c * ck:(c + 1) * ck, :])
        acc = part if acc is None else acc + part
    o_ref[...] = _layer_norm(alpha * hres + acc, lg_ref[...], lb_ref[...])


def _mlp(l, h, w, alpha, tm):
    t, d = h.shape
    row = pl.BlockSpec((tm, d), lambda i: (i, 0))
    weights = ((w["wup"], (d, D_FF)), (w["wdown"], (D_FF, d)), (w["lfg"], (1, d)),
               (w["lfb"], (1, d)))
    return pl.pallas_call(
        functools.partial(_mlp_kernel, alpha, 4),
        out_shape=jax.ShapeDtypeStruct((t, d), F32),
        grid=(t // tm,),
        in_specs=[row] + [_layer_spec(l, blk) for _, blk in weights],
        out_specs=row,
        compiler_params=_params(),
        name="mlp",
    )(h, *[a for a, _ in weights])


def _rot_cols(w):
    half = QK_ROPE // 2
    return jnp.concatenate([-w[..., half:], w[..., :half]], axis=-1)


def _prep(w_in, b_gate, conv_w, w_conv_out, q_norm_g, w_q_b, kv_norm_g, w_kv_b,
          w_mla_o, w_out, ln_mix_g, ln_mix_b, w_up, w_down, ln_ffn_g, ln_ffn_b):
    depth, d = w_in.shape[0], D_MODEL
    w_in_t = jnp.swapaxes(w_in, 1, 2)
    wy_t = w_in_t[:, OFF_KVA:OFF_KR, :]
    wyr_t = jnp.concatenate([-wy_t[:, QK_ROPE // 2:], wy_t[:, :QK_ROPE // 2]], axis=1)
    z64 = jnp.zeros((depth, QK_NOPE, d), F32)
    wkr_t = jnp.concatenate([z64, wy_t, wy_t, z64, wyr_t, wyr_t], axis=1)

    wq = w_q_b.reshape(depth, Q_LORA, N_HEADS, QK_NOPE + QK_ROPE)
    wqb = jnp.concatenate([wq, _rot_cols(wq[..., QK_NOPE:])], axis=-1)
    wqb = wqb.reshape(depth, Q_LORA, N_HEADS * HEAD_PAD)

    wkv = w_kv_b.reshape(depth, KV_LORA, N_HEADS, QK_NOPE + V_HEAD)
    wk = jnp.concatenate([wkv[..., :QK_NOPE], jnp.zeros_like(wkv[..., QK_NOPE:])], axis=-1)
    wk = wk.reshape(depth, KV_LORA, N_HEADS * HEAD_PAD)
    wvt = jnp.swapaxes(wkv[..., QK_NOPE:].reshape(depth, KV_LORA, N_HEADS * V_HEAD), 1, 2)

    bf = lambda a: a.astype(BF16)
    row = lambda a: a.reshape(depth, 1, -1)
    return {
        "w_main_t": bf(w_in_t), "wkr_t": bf(wkr_t),
        "wg_t": bf(w_in_t[:, OFF_KR:, :]), "bgate": row(b_gate),
        "qg": row(q_norm_g), "wqb": bf(wqb), "kvg": row(kv_norm_g),
        "wk": bf(wk), "wvt": bf(wvt),
        "convw": conv_w, "wco": bf(w_conv_out), "wmo": bf(w_mla_o), "wout": bf(w_out),
        "lmg": row(ln_mix_g), "lmb": row(ln_mix_b),
        "wup": bf(w_up), "wdown": bf(w_down),
        "lfg": row(ln_ffn_g), "lfb": row(ln_ffn_b),
    }


def kernel(x, positions, ln_in_g, ln_in_b, w_in, b_gate, conv_w, w_conv_out, q_norm_g, w_q_b, kv_norm_g, w_kv_b, w_mla_o, w_out, ln_mix_g, ln_mix_b, w_up, w_down, ln_ffn_g, ln_ffn_b):
    batch, seq, d = x.shape
    depth = w_in.shape[0]
    t = batch * seq
    alpha = (2 * depth) ** 0.25
    tm = 512
    tq = 256
    kc = 256
    lag = 3

    inv_freq = 1.0 / (ROPE_THETA ** (jnp.arange(0, QK_ROPE, 2, dtype=F32) / QK_ROPE))
    freq_row = jnp.concatenate([jnp.zeros((QK_NOPE,), F32)] + [inv_freq] * 4).reshape(1, HEAD_PAD)
    pos_b = jnp.broadcast_to(positions.reshape(t, 1), (t, HEAD_PAD))
    cos_t, sin_t = _rope_tables(pos_b, freq_row, tm)

    w = _prep(w_in, b_gate, conv_w, w_conv_out, q_norm_g, w_q_b, kv_norm_g, w_kv_b,
              w_mla_o, w_out, ln_mix_g, ln_mix_b, w_up, w_down, ln_ffn_g, ln_ffn_b)
    h = x.reshape(t, d)
    for l in range(depth):
        if l == 0:
            u, q, k, vt, h = _proj(l, h, cos_t, sin_t, w, seq, tm, tm,
                                   entry_ln=(ln_in_g.reshape(1, d), ln_in_b.reshape(1, d)))
        else:
            u, q, k, vt = _proj(l, h, cos_t, sin_t, w, seq, 2 * tm, tm)
        y = _attention(q, k, vt, seq, tq, 4, kc, lag)
        h = _mix(l, h, u, y, w, alpha, seq, 2 * tm, tm)
        h = _mlp(l, h, w, alpha, 2 * tm)
    return h.reshape(batch, seq, d)
```

```python
import functools
import math

import jax
import jax.numpy as jnp
from jax import lax
from jax.experimental import pallas as pl
from jax.experimental.pallas import tpu as pltpu

D_MODEL = 1024
N_HEADS = 16
QK_NOPE = 64
QK_ROPE = 32
V_HEAD = 64
Q_LORA = 768
KV_LORA = 256
ROPE_THETA = 10000.0
D_FF = 4 * D_MODEL
LN_EPS = 1e-5
RMS_EPS = 1e-6
HEAD_PAD = 128
SUBLANES = 8
BF16_SUBLANES = 16

OFF_B = D_MODEL
OFF_C = 2 * D_MODEL
OFF_H = 3 * D_MODEL
OFF_QA = OFF_H + Q_LORA
OFF_KVA = OFF_QA + KV_LORA
OFF_KR = OFF_KVA + QK_ROPE

VMEM_LIMIT_BYTES = 56 * 1024 * 1024

BF16 = jnp.bfloat16
F32 = jnp.float32

_NT = (((1,), (1,)), ((), ()))


def _dot(a, b):
    return jnp.dot(a, b, preferred_element_type=F32)


def _dot_t(a, bt):
    return lax.dot_general(a, bt, _NT, preferred_element_type=F32)


def _layer_norm(z, g, b):
    mu = jnp.mean(z, axis=-1, keepdims=True)
    zc = z - mu
    var = jnp.mean(zc * zc, axis=-1, keepdims=True)
    return zc * lax.rsqrt(var + LN_EPS) * g + b


def _rms_norm(z, g):
    ms = jnp.mean(z * z, axis=-1, keepdims=True)
    return z * lax.rsqrt(ms + RMS_EPS) * g


def _const_spec(shape):
    nd = len(shape)
    return pl.BlockSpec(shape, lambda *_: (0,) * nd, pipeline_mode=pl.Buffered(1))


def _layer_spec(l, block, row_block=0):
    return pl.BlockSpec((None,) + tuple(block), lambda *_: (l, row_block, 0),
                        pipeline_mode=pl.Buffered(1))


def _params():
    return pltpu.CompilerParams(
        dimension_semantics=("arbitrary",), vmem_limit_bytes=VMEM_LIMIT_BYTES)


def _rope_table_kernel(pos_ref, freq_ref, cos_ref, sin_ref):
    ang = pos_ref[...].astype(F32) * freq_ref[...]
    cos_ref[...] = jnp.cos(ang)
    sin_ref[...] = jnp.sin(ang)


def _rope_tables(pos_b, freq_row, tm):
    t = pos_b.shape[0]
    row = pl.BlockSpec((tm, HEAD_PAD), lambda i: (i, 0))
    return pl.pallas_call(
        _rope_table_kernel,
        out_shape=(jax.ShapeDtypeStruct((t, HEAD_PAD), F32),) * 2,
        grid=(t // tm,),
        in_specs=[row, _const_spec((1, HEAD_PAD))],
        out_specs=(row, row),
        compiler_params=_params(),
        name="rope_tables",
    )(pos_b, freq_row)


def _proj_kernel(q_scale, sub, entry_norm, h_ref, *refs):
    if entry_norm:
        lng_ref, lnb_ref, *refs, h0_ref = refs
    (cos_ref, sin_ref, wct_ref, wht_ref, wqat_ref, wkvat_ref, wkrt_ref, qg_ref, wqb_ref,
     kvg_ref, wk_ref, wvt_ref, u_ref, q_ref, k_ref, vt_ref) = refs
    for j in range(h_ref.shape[0] // sub):
        rows = slice(j * sub, (j + 1) * sub)
        hres = h_ref[rows, :]
        if entry_norm:
            hres = _layer_norm(hres, lng_ref[...], lnb_ref[...])
            h0_ref[rows, :] = hres
        x = hres.astype(BF16)
        cos = cos_ref[rows, :]
        sin = sin_ref[rows, :]

        u_ref[rows, :] = (_dot_t(x, wct_ref[...]) * _dot_t(x, wht_ref[...])).astype(BF16)

        lane = lax.broadcasted_iota(jnp.int32, cos.shape, 1)
        mq = jnp.where(lane < QK_NOPE, 1.0,
                       jnp.where(lane < QK_NOPE + QK_ROPE, cos, sin)) * q_scale
        mq2 = jnp.concatenate([mq, mq], axis=1)
        qn = _rms_norm(_dot_t(x, wqat_ref[...]), qg_ref[...]).astype(BF16)
        for hp in range(N_HEADS // 2):
            sl = slice(hp * 2 * HEAD_PAD, (hp + 1) * 2 * HEAD_PAD)
            q_ref[rows, sl] = (_dot(qn, wqb_ref[:, sl]) * mq2).astype(BF16)

        kr2 = _dot_t(x, wkrt_ref[...])
        rk = kr2[:, :HEAD_PAD] * cos + kr2[:, HEAD_PAD:] * sin
        rk2 = jnp.concatenate([rk, rk], axis=1)
        kvn = _rms_norm(_dot_t(x, wkvat_ref[...]), kvg_ref[...]).astype(BF16)
        for hp in range(N_HEADS // 2):
            sl = slice(hp * 2 * HEAD_PAD, (hp + 1) * 2 * HEAD_PAD)
            k_ref[rows, sl] = (_dot(kvn, wk_ref[:, sl]) + rk2).astype(BF16)

        vt_ref[0, :, rows] = lax.dot_general(wvt_ref[...], kvn, _NT,
                                             preferred_element_type=F32).astype(BF16)


def _proj(l, h, cos_t, sin_t, w, seq, tm, sub, entry_ln=None):
    t, d = h.shape
    nb = seq // tm
    hq = N_HEADS * HEAD_PAD
    hv = N_HEADS * V_HEAD
    row = lambda width: pl.BlockSpec((tm, width), lambda i: (i, 0))
    ln_args = () if entry_ln is None else tuple(entry_ln)
    ln_specs = [_const_spec(a.shape) for a in ln_args]
    extra_shape = () if entry_ln is None else (jax.ShapeDtypeStruct((t, d), F32),)
    extra_spec = () if entry_ln is None else (row(d),)
    weights = ((w["w_main_t"], (d, d), OFF_B // d), (w["w_main_t"], (d, d), OFF_C // d),
               (w["w_main_t"], (Q_LORA, d), OFF_H // Q_LORA),
               (w["w_main_t"], (KV_LORA, d), OFF_QA // KV_LORA),
               (w["wkr_t"], (2 * HEAD_PAD, d), 0), (w["qg"], (1, Q_LORA), 0),
               (w["wqb"], (Q_LORA, hq), 0), (w["kvg"], (1, KV_LORA), 0),
               (w["wk"], (KV_LORA, hq), 0), (w["wvt"], (hv, KV_LORA), 0))
    q_scale = (QK_NOPE + QK_ROPE) ** -0.5 * math.log2(math.e)
    return pl.pallas_call(
        functools.partial(_proj_kernel, q_scale, sub, entry_ln is not None),
        out_shape=(jax.ShapeDtypeStruct((t, d), BF16),
                   jax.ShapeDtypeStruct((t, hq), BF16),
                   jax.ShapeDtypeStruct((t, hq), BF16),
                   jax.ShapeDtypeStruct((t // seq, hv, seq), BF16)) + extra_shape,
        grid=(t // tm,),
        in_specs=[row(d)] + ln_specs + [row(HEAD_PAD), row(HEAD_PAD)]
        + [_layer_spec(l, blk, cb) for _, blk, cb in weights],
        out_specs=(row(d), row(hq), row(hq),
                   pl.BlockSpec((1, hv, tm), lambda i: (i // nb, 0, i % nb))) + extra_spec,
        compiler_params=_params(),
        name="proj",
    )(h, *ln_args, cos_t, sin_t, *[a for a, _, _ in weights])


def _row_max8(x):
    while x.shape[0] > SUBLANES:
        half = x.shape[0] // 2
        x = jnp.maximum(x[:half], x[half:])
    return x


def _attn_tile(kc, lag, row0, q_ref, k_ref, vt_ref, y_ref, s_buf, p_bufs):
    seq = k_ref.shape[0]
    tq = s_buf.shape[2]
    nc = seq // kc
    ring = s_buf.shape[0]
    rows = pl.ds(row0, tq)
    ones = jnp.ones((BF16_SUBLANES, kc), BF16)
    chunk_max, parts, outs, q_head = {}, {}, {}, {}

    def score_step(g, h, c):
        if c == 0:
            q_head.clear()
            q_head[h] = q_ref[rows, h * HEAD_PAD:(h + 1) * HEAD_PAD]
        k = k_ref[c * kc:(c + 1) * kc, h * HEAD_PAD:(h + 1) * HEAD_PAD]
        st = lax.dot_general(k, q_head[h], _NT, preferred_element_type=F32)
        s_buf[g % ring] = st
        chunk_max[g] = jnp.max(_row_max8(st), axis=0, keepdims=True)

    p_rows = pl.ds(pl.multiple_of(jnp.minimum(pl.program_id(0), 0), BF16_SUBLANES), kc)

    def exp_step(g, h, c):
        p_bufs[g % ring][p_rows, :] = jnp.exp2(s_buf[g % ring] - chunk_max[g]).astype(BF16)

    def value_step(g, h, c):
        v1 = jnp.concatenate(
            [vt_ref[0, h * V_HEAD:(h + 1) * V_HEAD, c * kc:(c + 1) * kc], ones], axis=0)
        parts[g] = _dot(v1, p_bufs[g % ring][p_rows, :])
        if c == nc - 1:
            gs = range(g - nc + 1, g + 1)
            m = functools.reduce(jnp.maximum, [chunk_max[i] for i in gs])
            ot = sum(parts.pop(i) * jnp.exp2(chunk_max.pop(i) - m) for i in gs)
            outs[h] = ot[:V_HEAD] / ot[V_HEAD:V_HEAD + 1]
            if h % 2 == 1:
                o2 = jnp.concatenate([outs.pop(h - 1), outs.pop(h)], axis=0)
                y_ref[rows, (h - 1) * V_HEAD:(h + 1) * V_HEAD] = o2.T.astype(BF16)

    total = N_HEADS * nc
    for step in range(total + 2 * lag):
        for stage, fn in enumerate((score_step, exp_step, value_step)):
            g = step - stage * lag
            if 0 <= g < total:
                fn(g, g // nc, g % nc)


def _attn_kernel(kc, lag, q_ref, k_ref, vt_ref, y_ref, s_buf, *p_bufs):
    tq = s_buf.shape[2]

    def tile(j, carry):
        _attn_tile(kc, lag, pl.multiple_of(j * tq, tq), q_ref, k_ref, vt_ref, y_ref,
                   s_buf, p_bufs)
        return carry

    lax.fori_loop(0, q_ref.shape[0] // tq, tile, 0)


def _attention(q, k, vt, seq, tq, n_sub, kc, lag):
    t, hq = q.shape
    tb = tq * n_sub
    nb = seq // tb
    hv = N_HEADS * V_HEAD
    ring = lag + 1
    return pl.pallas_call(
        functools.partial(_attn_kernel, kc, lag),
        out_shape=jax.ShapeDtypeStruct((t, hv), BF16),
        grid=(t // tb,),
        in_specs=[pl.BlockSpec((tb, hq), lambda i: (i, 0)),
                  pl.BlockSpec((seq, hq), lambda i: (i // nb, 0)),
                  pl.BlockSpec((1, hv, seq), lambda i: (i // nb, 0, 0))],
        out_specs=pl.BlockSpec((tb, hv), lambda i: (i, 0)),
        scratch_shapes=[pltpu.VMEM((ring, kc, tq), F32)] + [pltpu.VMEM((kc, tq), BF16)] * ring,
        compiler_params=_params(),
        name="attention",
    )(q, k, vt)


def _mix_kernel(alpha, nb, sub, h_ref, u_ref, up_ref, un_ref, y_ref, wbt_ref, wgt_ref, bg_ref,
                cw_ref, wco_ref, wmo_ref, wo_ref, lg_ref, lb_ref, o_ref):
    i = pl.program_id(0)
    tm = h_ref.shape[0]
    halo = up_ref.shape[0]
    has_prev = (i % nb != 0).astype(F32)
    has_next = (i % nb != nb - 1).astype(F32)
    cw = cw_ref[...]
    for j in range(tm // sub):
        lo, hi = j * sub, (j + 1) * sub
        hres = h_ref[lo:hi, :]
        x = hres.astype(BF16)

        u = u_ref[lo:hi, :].astype(F32)
        if j == 0:
            prev_row = up_ref[halo - 1:halo, :].astype(F32) * has_prev
        else:
            prev_row = u_ref[lo - 1:lo, :].astype(F32)
        if hi == tm:
            next_row = un_ref[0:1, :].astype(F32) * has_next
        else:
            next_row = u_ref[hi:hi + 1, :].astype(F32)
        r = lax.broadcasted_iota(jnp.int32, u.shape, 0)
        u_m1 = jnp.where(r == 0, prev_row, pltpu.roll(u, 1, axis=0))
        u_p1 = jnp.where(r == sub - 1, next_row, pltpu.roll(u, sub - 1, axis=0))
        conv = cw[0:1, :] * u_m1 + cw[1:2, :] * u + cw[2:3, :] * u_p1

        yc = _dot((_dot_t(x, wbt_ref[...]) * conv).astype(BF16), wco_ref[...])
        ym = _dot(y_ref[lo:hi, :], wmo_ref[...])
        g = jax.nn.sigmoid(_dot_t(x, wgt_ref[...]) + bg_ref[...])
        merged = (g[:, :D_MODEL] * yc + g[:, D_MODEL:] * ym).astype(BF16)
        mix = _dot(merged, wo_ref[...])
        o_ref[lo:hi, :] = _layer_norm(alpha * hres + mix, lg_ref[...], lb_ref[...])


def _mix(l, h, u, y, w, alpha, seq, tm, sub):
    t, d = h.shape
    nb = seq // tm
    halo = BF16_SUBLANES
    hb = tm // halo
    nhalo = t // halo
    row = lambda width: pl.BlockSpec((tm, width), lambda i: (i, 0))
    weights = ((w["w_main_t"], (d, d), 0), (w["wg_t"], (2 * d, d), 0), (w["bgate"], (1, 2 * d), 0),
               (w["convw"], (3, d), 0), (w["wco"], (d, d), 0), (w["wmo"], (d, d), 0),
               (w["wout"], (d, d), 0), (w["lmg"], (1, d), 0), (w["lmb"], (1, d), 0))
    return pl.pallas_call(
        functools.partial(_mix_kernel, alpha, nb, sub),
        out_shape=jax.ShapeDtypeStruct((t, d), F32),
        grid=(t // tm,),
        in_specs=[row(d), row(d),
                  pl.BlockSpec((halo, d), lambda i: (jnp.maximum(i * hb - 1, 0), 0)),
                  pl.BlockSpec((halo, d), lambda i: (jnp.minimum((i + 1) * hb, nhalo - 1), 0)),
                  row(d)] + [_layer_spec(l, blk, cb) for _, blk, cb in weights],
        out_specs=row(d),
        compiler_params=_params(),
        name="mix",
    )(h, u, u, u, y, *[a for a, _, _ in weights])


def _mlp_kernel(alpha, n_chunks, h_ref, wu_ref, wd_ref, lg_ref, lb_ref, o_ref):
    hres = h_ref[...]
    x = hres.astype(BF16)
    ck = D_FF // n_chunks
    acc = None
    for c in range(n_chunks):
        a = jnp.maximum(_dot(x, wu_ref[:, c * ck:(c + 1) * ck]), 0.0)
        part = _dot((a * a).astype(BF16), wd_ref[c * ck:(c + 1) * ck, :])
        acc = part if acc is None else acc + part
    o_ref[...] = _layer_norm(alpha * hres + acc, lg_ref[...], lb_ref[...])


def _mlp(l, h, w, alpha, tm):
    t, d = h.shape
    row = pl.BlockSpec((tm, d), lambda i: (i, 0))
    weights = ((w["wup"], (d, D_FF)), (w["wdown"], (D_FF, d)), (w["lfg"], (1, d)),
               (w["lfb"], (1, d)))
    return pl.pallas_call(
        functools.partial(_mlp_kernel, alpha, 4),
        out_shape=jax.ShapeDtypeStruct((t, d), F32),
        grid=(t // tm,),
        in_specs=[row] + [_layer_spec(l, blk) for _, blk in weights],
        out_specs=row,
        compiler_params=_params(),
        name="mlp",
    )(h, *[a for a, _ in weights])


def _rot_cols(w):
    half = QK_ROPE // 2
    return jnp.concatenate([-w[..., half:], w[..., :half]], axis=-1)


def _prep(w_in, b_gate, conv_w, w_conv_out, q_norm_g, w_q_b, kv_norm_g, w_kv_b,
          w_mla_o, w_out, ln_mix_g, ln_mix_b, w_up, w_down, ln_ffn_g, ln_ffn_b):
    depth, d = w_in.shape[0], D_MODEL
    w_in_t = jnp.swapaxes(w_in, 1, 2)
    wy_t = w_in_t[:, OFF_KVA:OFF_KR, :]
    wyr_t = jnp.concatenate([-wy_t[:, QK_ROPE // 2:], wy_t[:, :QK_ROPE // 2]], axis=1)
    z64 = jnp.zeros((depth, QK_NOPE, d), F32)
    wkr_t = jnp.concatenate([z64, wy_t, wy_t, z64, wyr_t, wyr_t], axis=1)

    wq = w_q_b.reshape(depth, Q_LORA, N_HEADS, QK_NOPE + QK_ROPE)
    wqb = jnp.concatenate([wq, _rot_cols(wq[..., QK_NOPE:])], axis=-1)
    wqb = wqb.reshape(depth, Q_LORA, N_HEADS * HEAD_PAD)

    wkv = w_kv_b.reshape(depth, KV_LORA, N_HEADS, QK_NOPE + V_HEAD)
    wk = jnp.concatenate([wkv[..., :QK_NOPE], jnp.zeros_like(wkv[..., QK_NOPE:])], axis=-1)
    wk = wk.reshape(depth, KV_LORA, N_HEADS * HEAD_PAD)
    wvt = jnp.swapaxes(wkv[..., QK_NOPE:].reshape(depth, KV_LORA, N_HEADS * V_HEAD), 1, 2)

    bf = lambda a: a.astype(BF16)
    row = lambda a: a.reshape(depth, 1, -1)
    return {
        "w_main_t": bf(w_in_t), "wkr_t": bf(wkr_t),
        "wg_t": bf(w_in_t[:, OFF_KR:, :]), "bgate": row(b_gate),
        "qg": row(q_norm_g), "wqb": bf(wqb), "kvg": row(kv_norm_g),
        "wk": bf(wk), "wvt": bf(wvt),
        "convw": conv_w, "wco": bf(w_conv_out), "wmo": bf(w_mla_o), "wout": bf(w_out),
        "lmg": row(ln_mix_g), "lmb": row(ln_mix_b),
        "wup": bf(w_up), "wdown": bf(w_down),
        "lfg": row(ln_ffn_g), "lfb": row(ln_ffn_b),
    }


def kernel(x, positions, ln_in_g, ln_in_b, w_in, b_gate, conv_w, w_conv_out, q_norm_g, w_q_b, kv_norm_g, w_kv_b, w_mla_o, w_out, ln_mix_g, ln_mix_b, w_up, w_down, ln_ffn_g, ln_ffn_b):
    batch, seq, d = x.shape
    depth = w_in.shape[0]
    t = batch * seq
    alpha = (2 * depth) ** 0.25
    tm = 512
    tq = 256
    kc = 256
    lag = 3

    inv_freq = 1.0 / (ROPE_THETA ** (jnp.arange(0, QK_ROPE, 2, dtype=F32) / QK_ROPE))
    freq_row = jnp.concatenate([jnp.zeros((QK_NOPE,), F32)] + [inv_freq] * 4).reshape(1, HEAD_PAD)
    pos_b = jnp.broadcast_to(positions.reshape(t, 1), (t, HEAD_PAD))
    cos_t, sin_t = _rope_tables(pos_b, freq_row, tm)

    w = _prep(w_in, b_gate, conv_w, w_conv_out, q_norm_g, w_q_b, kv_norm_g, w_kv_b,
              w_mla_o, w_out, ln_mix_g, ln_mix_b, w_up, w_down, ln_ffn_g, ln_ffn_b)
    h = x.reshape(t, d)
    for l in range(depth):
        if l == 0:
            u, q, k, vt, h = _proj(l, h, cos_t, sin_t, w, seq, tm, tm,
                                   entry_ln=(ln_in_g.reshape(1, d), ln_in_b.reshape(1, d)))
        else:
            u, q, k, vt = _proj(l, h, cos_t, sin_t, w, seq, 2 * tm, tm)
        y = _attention(q, k, vt, seq, tq, 4, kc, lag)
        h = _mix(l, h, u, y, w, alpha, seq, 2 * tm, tm)
        h = _mlp(l, h, w, alpha, 2 * tm)
    return h.reshape(batch, seq, d)
```

```python
import functools
import math

import jax
import jax.numpy as jnp
from jax import lax
from jax.experimental import pallas as pl
from jax.experimental.pallas import tpu as pltpu

D_MODEL = 1024
N_HEADS = 16
QK_NOPE = 64
QK_ROPE = 32
V_HEAD = 64
Q_LORA = 768
KV_LORA = 256
ROPE_THETA = 10000.0
D_FF = 4 * D_MODEL
LN_EPS = 1e-5
RMS_EPS = 1e-6
HEAD_PAD = 128
SUBLANES = 8
BF16_SUBLANES = 16

OFF_B = D_MODEL
OFF_C = 2 * D_MODEL
OFF_H = 3 * D_MODEL
OFF_QA = OFF_H + Q_LORA
OFF_KVA = OFF_QA + KV_LORA
OFF_KR = OFF_KVA + QK_ROPE

VMEM_LIMIT_BYTES = 56 * 1024 * 1024

BF16 = jnp.bfloat16
F32 = jnp.float32

_NT = (((1,), (1,)), ((), ()))


def _dot(a, b):
    return jnp.dot(a, b, preferred_element_type=F32)


def _dot_t(a, bt):
    return lax.dot_general(a, bt, _NT, preferred_element_type=F32)


def _layer_norm(z, g, b):
    mu = jnp.mean(z, axis=-1, keepdims=True)
    zc = z - mu
    var = jnp.mean(zc * zc, axis=-1, keepdims=True)
    return zc * lax.rsqrt(var + LN_EPS) * g + b


def _rms_norm(z, g):
    ms = jnp.mean(z * z, axis=-1, keepdims=True)
    return z * lax.rsqrt(ms + RMS_EPS) * g


def _const_spec(shape):
    nd = len(shape)
    return pl.BlockSpec(shape, lambda *_: (0,) * nd, pipeline_mode=pl.Buffered(1))


def _layer_spec(l, block, row_block=0):
    return pl.BlockSpec((None,) + tuple(block), lambda *_: (l, row_block, 0),
                        pipeline_mode=pl.Buffered(1))


def _params():
    return pltpu.CompilerParams(
        dimension_semantics=("arbitrary",), vmem_limit_bytes=VMEM_LIMIT_BYTES)


def _rope_table_kernel(pos_ref, freq_ref, cos_ref, sin_ref):
    ang = freq_ref[...] * pos_ref[...].astype(F32)
    cos = jnp.cos(ang)
    sin = jnp.sin(ang)
    pad = (QK_NOPE, ang.shape[1])
    cos_ref[...] = jnp.concatenate([jnp.ones(pad, F32)] + [cos] * 4, axis=0).T
    sin_ref[...] = jnp.concatenate([jnp.zeros(pad, F32)] + [sin] * 4, axis=0).T


def _rope_tables(pos_row, freq_col, tm):
    t = pos_row.shape[1]
    out = pl.BlockSpec((tm, HEAD_PAD), lambda i: (i, 0))
    return pl.pallas_call(
        _rope_table_kernel,
        out_shape=(jax.ShapeDtypeStruct((t, HEAD_PAD), F32),) * 2,
        grid=(t // tm,),
        in_specs=[pl.BlockSpec((1, tm), lambda i: (0, i)), _const_spec(freq_col.shape)],
        out_specs=(out, out),
        compiler_params=_params(),
        name="rope_tables",
    )(pos_row, freq_col)


def _proj_kernel(q_scale, sub, entry_norm, h_ref, *refs):
    if entry_norm:
        lng_ref, lnb_ref, *refs, h0_ref = refs
    (cos_ref, sin_ref, wct_ref, wht_ref, wqat_ref, wkvat_ref, wkrt_ref, qg_ref, wqb_ref,
     kvg_ref, wk_ref, wvt_ref, u_ref, q_ref, k_ref, vt_ref) = refs
    for j in range(h_ref.shape[0] // sub):
        rows = slice(j * sub, (j + 1) * sub)
        hres = h_ref[rows, :]
        if entry_norm:
            hres = _layer_norm(hres, lng_ref[...], lnb_ref[...])
            h0_ref[rows, :] = hres
        x = hres.astype(BF16)
        cos = cos_ref[rows, :]
        sin = sin_ref[rows, :]

        u_ref[rows, :] = (_dot_t(x, wct_ref[...]) * _dot_t(x, wht_ref[...])).astype(BF16)

        lane = lax.broadcasted_iota(jnp.int32, cos.shape, 1)
        mq = jnp.where(lane < QK_NOPE, 1.0,
                       jnp.where(lane < QK_NOPE + QK_ROPE, cos, sin)) * q_scale
        mq2 = jnp.concatenate([mq, mq], axis=1)
        qn = _rms_norm(_dot_t(x, wqat_ref[...]), qg_ref[...]).astype(BF16)
        for hp in range(N_HEADS // 2):
            sl = slice(hp * 2 * HEAD_PAD, (hp + 1) * 2 * HEAD_PAD)
            q_ref[rows, sl] = (_dot(qn, wqb_ref[:, sl]) * mq2).astype(BF16)

        kr2 = _dot_t(x, wkrt_ref[...])
        rk = kr2[:, :HEAD_PAD] * cos + kr2[:, HEAD_PAD:] * sin
        rk2 = jnp.concatenate([rk, rk], axis=1)
        kvn = _rms_norm(_dot_t(x, wkvat_ref[...]), kvg_ref[...]).astype(BF16)
        for hp in range(N_HEADS // 2):
            sl = slice(hp * 2 * HEAD_PAD, (hp + 1) * 2 * HEAD_PAD)
            k_ref[rows, sl] = (_dot(kvn, wk_ref[:, sl]) + rk2).astype(BF16)

        vt_ref[0, :, rows] = lax.dot_general(wvt_ref[...], kvn, _NT,
                                             preferred_element_type=F32).astype(BF16)


def _proj(l, h, cos_t, sin_t, w, seq, tm, sub, entry_ln=None):
    t, d = h.shape
    nb = seq // tm
    hq = N_HEADS * HEAD_PAD
    hv = N_HEADS * V_HEAD
    row = lambda width: pl.BlockSpec((tm, width), lambda i: (i, 0))
    ln_args = () if entry_ln is None else tuple(entry_ln)
    ln_specs = [_const_spec(a.shape) for a in ln_args]
    extra_shape = () if entry_ln is None else (jax.ShapeDtypeStruct((t, d), F32),)
    extra_spec = () if entry_ln is None else (row(d),)
    weights = ((w["w_main_t"], (d, d), OFF_B // d), (w["w_main_t"], (d, d), OFF_C // d),
               (w["w_main_t"], (Q_LORA, d), OFF_H // Q_LORA),
               (w["w_main_t"], (KV_LORA, d), OFF_QA // KV_LORA),
               (w["wkr_t"], (2 * HEAD_PAD, d), 0), (w["qg"], (1, Q_LORA), 0),
               (w["wqb"], (Q_LORA, hq), 0), (w["kvg"], (1, KV_LORA), 0),
               (w["wk"], (KV_LORA, hq), 0), (w["wvt"], (hv, KV_LORA), 0))
    q_scale = (QK_NOPE + QK_ROPE) ** -0.5 * math.log2(math.e)
    return pl.pallas_call(
        functools.partial(_proj_kernel, q_scale, sub, entry_ln is not None),
        out_shape=(jax.ShapeDtypeStruct((t, d), BF16),
                   jax.ShapeDtypeStruct((t, hq), BF16),
                   jax.ShapeDtypeStruct((t, hq), BF16),
                   jax.ShapeDtypeStruct((t // seq, hv, seq), BF16)) + extra_shape,
        grid=(t // tm,),
        in_specs=[row(d)] + ln_specs + [row(HEAD_PAD), row(HEAD_PAD)]
        + [_layer_spec(l, blk, cb) for _, blk, cb in weights],
        out_specs=(row(d), row(hq), row(hq),
                   pl.BlockSpec((1, hv, tm), lambda i: (i // nb, 0, i % nb))) + extra_spec,
        compiler_params=_params(),
        name="proj",
    )(h, *ln_args, cos_t, sin_t, *[a for a, _, _ in weights])


def _row_max8(x):
    while x.shape[0] > SUBLANES:
        half = x.shape[0] // 2
        x = jnp.maximum(x[:half], x[half:])
    return x


def _attn_tile(kc, lag, row0, q_ref, k_ref, vt_ref, y_ref, s_buf, p_bufs):
    seq = k_ref.shape[0]
    tq = s_buf.shape[2]
    nc = seq // kc
    ring = s_buf.shape[0]
    rows = pl.ds(row0, tq)
    ones = jnp.ones((BF16_SUBLANES, kc), BF16)
    chunk_max, parts, outs, q_head = {}, {}, {}, {}

    def score_step(g, h, c):
        if c == 0:
            q_head.clear()
            q_head[h] = q_ref[rows, h * HEAD_PAD:(h + 1) * HEAD_PAD]
        k = k_ref[c * kc:(c + 1) * kc, h * HEAD_PAD:(h + 1) * HEAD_PAD]
        st = lax.dot_general(k, q_head[h], _NT, preferred_element_type=F32)
        s_buf[g % ring] = st
        chunk_max[g] = jnp.max(_row_max8(st), axis=0, keepdims=True)

    p_rows = pl.ds(pl.multiple_of(jnp.minimum(pl.program_id(0), 0), BF16_SUBLANES), kc)

    def exp_step(g, h, c):
        p_bufs[g % ring][p_rows, :] = jnp.exp2(s_buf[g % ring] - chunk_max[g]).astype(BF16)

    def value_step(g, h, c):
        v1 = jnp.concatenate(
            [vt_ref[0, h * V_HEAD:(h + 1) * V_HEAD, c * kc:(c + 1) * kc], ones], axis=0)
        parts[g] = _dot(v1, p_bufs[g % ring][p_rows, :])
        if c == nc - 1:
            gs = range(g - nc + 1, g + 1)
            m = functools.reduce(jnp.maximum, [chunk_max[i] for i in gs])
            ot = sum(parts.pop(i) * jnp.exp2(chunk_max.pop(i) - m) for i in gs)
            outs[h] = ot[:V_HEAD] / ot[V_HEAD:V_HEAD + 1]
            if h % 2 == 1:
                o2 = jnp.concatenate([outs.pop(h - 1), outs.pop(h)], axis=0)
                y_ref[rows, (h - 1) * V_HEAD:(h + 1) * V_HEAD] = o2.T.astype(BF16)

    total = N_HEADS * nc
    for step in range(total + 2 * lag):
        for stage, fn in enumerate((score_step, exp_step, value_step)):
            g = step - stage * lag
            if 0 <= g < total:
                fn(g, g // nc, g % nc)


def _attn_kernel(kc, lag, q_ref, k_ref, vt_ref, y_ref, s_buf, *p_bufs):
    tq = s_buf.shape[2]

    def tile(j, carry):
        _attn_tile(kc, lag, pl.multiple_of(j * tq, tq), q_ref, k_ref, vt_ref, y_ref,
                   s_buf, p_bufs)
        return carry

    lax.fori_loop(0, q_ref.shape[0] // tq, tile, 0)


def _attention(q, k, vt, seq, tq, n_sub, kc, lag):
    t, hq = q.shape
    tb = tq * n_sub
    nb = seq // tb
    hv = N_HEADS * V_HEAD
    ring = lag + 1
    return pl.pallas_call(
        functools.partial(_attn_kernel, kc, lag),
        out_shape=jax.ShapeDtypeStruct((t, hv), BF16),
        grid=(t // tb,),
        in_specs=[pl.BlockSpec((tb, hq), lambda i: (i, 0)),
                  pl.BlockSpec((seq, hq), lambda i: (i // nb, 0)),
                  pl.BlockSpec((1, hv, seq), lambda i: (i // nb, 0, 0))],
        out_specs=pl.BlockSpec((tb, hv), lambda i: (i, 0)),
        scratch_shapes=[pltpu.VMEM((ring, kc, tq), F32)] + [pltpu.VMEM((kc, tq), BF16)] * ring,
        compiler_params=_params(),
        name="attention",
    )(q, k, vt)


def _mix_kernel(alpha, nb, sub, h_ref, u_ref, up_ref, un_ref, y_ref, wbt_ref, wgt_ref, bg_ref,
                cw_ref, wco_ref, wmo_ref, wo_ref, lg_ref, lb_ref, o_ref):
    i = pl.program_id(0)
    tm = h_ref.shape[0]
    halo = up_ref.shape[0]
    has_prev = (i % nb != 0).astype(F32)
    has_next = (i % nb != nb - 1).astype(F32)
    cw = cw_ref[...]
    for j in range(tm // sub):
        lo, hi = j * sub, (j + 1) * sub
        hres = h_ref[lo:hi, :]
        x = hres.astype(BF16)

        u = u_ref[lo:hi, :].astype(F32)
        if j == 0:
            prev_row = up_ref[halo - 1:halo, :].astype(F32) * has_prev
        else:
            prev_row = u_ref[lo - 1:lo, :].astype(F32)
        if hi == tm:
            next_row = un_ref[0:1, :].astype(F32) * has_next
        else:
            next_row = u_ref[hi:hi + 1, :].astype(F32)
        r = lax.broadcasted_iota(jnp.int32, u.shape, 0)
        u_m1 = jnp.where(r == 0, prev_row, pltpu.roll(u, 1, axis=0))
        u_p1 = jnp.where(r == sub - 1, next_row, pltpu.roll(u, sub - 1, axis=0))
        conv = cw[0:1, :] * u_m1 + cw[1:2, :] * u + cw[2:3, :] * u_p1

        yc = _dot((_dot_t(x, wbt_ref[...]) * conv).astype(BF16), wco_ref[...])
        ym = _dot(y_ref[lo:hi, :], wmo_ref[...])
        g = jax.nn.sigmoid(_dot_t(x, wgt_ref[...]) + bg_ref[...])
        merged = (g[:, :D_MODEL] * yc + g[:, D_MODEL:] * ym).astype(BF16)
        mix = _dot(merged, wo_ref[...])
        o_ref[lo:hi, :] = _layer_norm(alpha * hres + mix, lg_ref[...], lb_ref[...])


def _mix(l, h, u, y, w, alpha, seq, tm, sub):
    t, d = h.shape
    nb = seq // tm
    halo = BF16_SUBLANES
    hb = tm // halo
    nhalo = t // halo
    row = lambda width: pl.BlockSpec((tm, width), lambda i: (i, 0))
    weights = ((w["w_main_t"], (d, d), 0), (w["wg_t"], (2 * d, d), 0), (w["bgate"], (1, 2 * d), 0),
               (w["convw"], (3, d), 0), (w["wco"], (d, d), 0), (w["wmo"], (d, d), 0),
               (w["wout"], (d, d), 0), (w["lmg"], (1, d), 0), (w["lmb"], (1, d), 0))
    return pl.pallas_call(
        functools.partial(_mix_kernel, alpha, nb, sub),
        out_shape=jax.ShapeDtypeStruct((t, d), F32),
        grid=(t // tm,),
        in_specs=[row(d), row(d),
                  pl.BlockSpec((halo, d), lambda i: (jnp.maximum(i * hb - 1, 0), 0)),
                  pl.BlockSpec((halo, d), lambda i: (jnp.minimum((i + 1) * hb, nhalo - 1), 0)),
                  row(d)] + [_layer_spec(l, blk, cb) for _, blk, cb in weights],
        out_specs=row(d),
        compiler_params=_params(),
        name="mix",
    )(h, u, u, u, y, *[a for a, _, _ in weights])


def _mlp_kernel(alpha, n_chunks, h_ref, wu_ref, wd_ref, lg_ref, lb_ref, o_ref):
    hres = h_ref[...]
    x = hres.astype(BF16)
    ck = D_FF // n_chunks
    acc = None
    for c in range(n_chunks):
        a = jnp.maximum(_dot(x, wu_ref[:, c * ck:(c + 1) * ck]), 0.0)
        part = _dot((a * a).astype(BF16), wd_ref[c * ck:(c + 1) * ck, :])
        acc = part if acc is None else acc + part
    o_ref[...] = _layer_norm(alpha * hres + acc, lg_ref[...], lb_ref[...])


def _mlp(l, h, w, alpha, tm):
    t, d = h.shape
    row = pl.BlockSpec((tm, d), lambda i: (i, 0))
    weights = ((w["wup"], (d, D_FF)), (w["wdown"], (D_FF, d)), (w["lfg"], (1, d)),
               (w["lfb"], (1, d)))
    return pl.pallas_call(
        functools.partial(_mlp_kernel, alpha, 4),
        out_shape=jax.ShapeDtypeStruct((t, d), F32),
        grid=(t // tm,),
        in_specs=[row] + [_layer_spec(l, blk) for _, blk in weights],
        out_specs=row,
        compiler_params=_params(),
        name="mlp",
    )(h, *[a for a, _ in weights])


def _rot_cols(w):
    half = QK_ROPE // 2
    return jnp.concatenate([-w[..., half:], w[..., :half]], axis=-1)


def _prep(w_in, b_gate, conv_w, w_conv_out, q_norm_g, w_q_b, kv_norm_g, w_kv_b,
          w_mla_o, w_out, ln_mix_g, ln_mix_b, w_up, w_down, ln_ffn_g, ln_ffn_b):
    depth, d = w_in.shape[0], D_MODEL
    w_in_t = jnp.swapaxes(w_in, 1, 2)
    wy_t = w_in_t[:, OFF_KVA:OFF_KR, :]
    wyr_t = jnp.concatenate([-wy_t[:, QK_ROPE // 2:], wy_t[:, :QK_ROPE // 2]], axis=1)
    z64 = jnp.zeros((depth, QK_NOPE, d), F32)
    wkr_t = jnp.concatenate([z64, wy_t, wy_t, z64, wyr_t, wyr_t], axis=1)

    wq = w_q_b.reshape(depth, Q_LORA, N_HEADS, QK_NOPE + QK_ROPE)
    wqb = jnp.concatenate([wq, _rot_cols(wq[..., QK_NOPE:])], axis=-1)
    wqb = wqb.reshape(depth, Q_LORA, N_HEADS * HEAD_PAD)

    wkv = w_kv_b.reshape(depth, KV_LORA, N_HEADS, QK_NOPE + V_HEAD)
    wk = jnp.concatenate([wkv[..., :QK_NOPE], jnp.zeros_like(wkv[..., QK_NOPE:])], axis=-1)
    wk = wk.reshape(depth, KV_LORA, N_HEADS * HEAD_PAD)
    wvt = jnp.swapaxes(wkv[..., QK_NOPE:].reshape(depth, KV_LORA, N_HEADS * V_HEAD), 1, 2)

    bf = lambda a: a.astype(BF16)
    row = lambda a: a.reshape(depth, 1, -1)
    return {
        "w_main_t": bf(w_in_t), "wkr_t": bf(wkr_t),
        "wg_t": bf(w_in_t[:, OFF_KR:, :]), "bgate": row(b_gate),
        "qg": row(q_norm_g), "wqb": bf(wqb), "kvg": row(kv_norm_g),
        "wk": bf(wk), "wvt": bf(wvt),
        "convw": conv_w, "wco": bf(w_conv_out), "wmo": bf(w_mla_o), "wout": bf(w_out),
        "lmg": row(ln_mix_g), "lmb": row(ln_mix_b),
        "wup": bf(w_up), "wdown": bf(w_down),
        "lfg": row(ln_ffn_g), "lfb": row(ln_ffn_b),
    }


def kernel(x, positions, ln_in_g, ln_in_b, w_in, b_gate, conv_w, w_conv_out, q_norm_g, w_q_b, kv_norm_g, w_kv_b, w_mla_o, w_out, ln_mix_g, ln_mix_b, w_up, w_down, ln_ffn_g, ln_ffn_b):
    batch, seq, d = x.shape
    depth = w_in.shape[0]
    t = batch * seq
    alpha = (2 * depth) ** 0.25
    tm = 512
    tq = 256
    kc = 256
    lag = 3

    inv_freq = 1.0 / (ROPE_THETA ** (jnp.arange(0, QK_ROPE, 2, dtype=F32) / QK_ROPE))
    cos_t, sin_t = _rope_tables(positions.reshape(1, t), inv_freq.reshape(-1, 1), tm)

    w = _prep(w_in, b_gate, conv_w, w_conv_out, q_norm_g, w_q_b, kv_norm_g, w_kv_b,
              w_mla_o, w_out, ln_mix_g, ln_mix_b, w_up, w_down, ln_ffn_g, ln_ffn_b)
    h = x.reshape(t, d)
    for l in range(depth):
        if l == 0:
            u, q, k, vt, h = _proj(l, h, cos_t, sin_t, w, seq, tm, tm,
                                   entry_ln=(ln_in_g.reshape(1, d), ln_in_b.reshape(1, d)))
        else:
            u, q, k, vt = _proj(l, h, cos_t, sin_t, w, seq, 2 * tm, 2 * tm)
        y = _attention(q, k, vt, seq, tq, 4, kc, lag)
        h = _mix(l, h, u, y, w, alpha, seq, 2 * tm, 2 * tm)
        h = _mlp(l, h, w, alpha, 2 * tm)
    return h.reshape(batch, seq, d)
```

```python
import functools
import math

import jax
import jax.numpy as jnp
from jax import lax
from jax.experimental import pallas as pl
from jax.experimental.pallas import tpu as pltpu

D_MODEL = 1024
N_HEADS = 16
QK_NOPE = 64
QK_ROPE = 32
V_HEAD = 64
Q_LORA = 768
KV_LORA = 256
ROPE_THETA = 10000.0
D_FF = 4 * D_MODEL
LN_EPS = 1e-5
RMS_EPS = 1e-6
HEAD_PAD = 128
SUBLANES = 8
BF16_SUBLANES = 16

OFF_B = D_MODEL
OFF_C = 2 * D_MODEL
OFF_H = 3 * D_MODEL
OFF_QA = OFF_H + Q_LORA
OFF_KVA = OFF_QA + KV_LORA
OFF_KR = OFF_KVA + QK_ROPE

VMEM_LIMIT_BYTES = 56 * 1024 * 1024

BF16 = jnp.bfloat16
F32 = jnp.float32

_NT = (((1,), (1,)), ((), ()))


def _dot(a, b):
    return jnp.dot(a, b, preferred_element_type=F32)


def _dot_t(a, bt):
    return lax.dot_general(a, bt, _NT, preferred_element_type=F32)


def _layer_norm(z, g, b):
    mu = jnp.mean(z, axis=-1, keepdims=True)
    zc = z - mu
    var = jnp.mean(zc * zc, axis=-1, keepdims=True)
    return zc * lax.rsqrt(var + LN_EPS) * g + b


def _rms_norm(z, g):
    ms = jnp.mean(z * z, axis=-1, keepdims=True)
    return z * lax.rsqrt(ms + RMS_EPS) * g


def _const_spec(shape):
    nd = len(shape)
    return pl.BlockSpec(shape, lambda *_: (0,) * nd, pipeline_mode=pl.Buffered(1))


def _layer_spec(l, block, row_block=0):
    return pl.BlockSpec((None,) + tuple(block), lambda *_: (l, row_block, 0),
                        pipeline_mode=pl.Buffered(1))


def _params():
    return pltpu.CompilerParams(
        dimension_semantics=("arbitrary",), vmem_limit_bytes=VMEM_LIMIT_BYTES)


def _rope_table_kernel(pos_ref, freq_ref, cos_ref, sin_ref):
    ang = freq_ref[...] * pos_ref[...].astype(F32)
    cos = jnp.cos(ang)
    sin = jnp.sin(ang)
    pad = (QK_NOPE, ang.shape[1])
    cos_ref[...] = jnp.concatenate([jnp.ones(pad, F32)] + [cos] * 4, axis=0).T
    sin_ref[...] = jnp.concatenate([jnp.zeros(pad, F32)] + [sin] * 4, axis=0).T


def _rope_tables(pos_row, freq_col, tm):
    t = pos_row.shape[1]
    out = pl.BlockSpec((tm, HEAD_PAD), lambda i: (i, 0))
    return pl.pallas_call(
        _rope_table_kernel,
        out_shape=(jax.ShapeDtypeStruct((t, HEAD_PAD), F32),) * 2,
        grid=(t // tm,),
        in_specs=[pl.BlockSpec((1, tm), lambda i: (0, i)), _const_spec(freq_col.shape)],
        out_specs=(out, out),
        compiler_params=_params(),
        name="rope_tables",
    )(pos_row, freq_col)


def _proj_kernel(q_scale, sub, entry_norm, h_ref, *refs):
    if entry_norm:
        lng_ref, lnb_ref, *refs = refs
    (cos_ref, sin_ref, wct_ref, wht_ref, wqat_ref, wkvat_ref, wkrt_ref, qg_ref, wqb_ref,
     kvg_ref, wk_ref, wvt_ref, u_ref, q_ref, k_ref, vt_ref) = refs
    for j in range(h_ref.shape[0] // sub):
        rows = slice(j * sub, (j + 1) * sub)
        hres = h_ref[rows, :]
        if entry_norm:
            hres = _layer_norm(hres, lng_ref[...], lnb_ref[...])
        x = hres.astype(BF16)
        cos = cos_ref[rows, :]
        sin = sin_ref[rows, :]

        u_ref[rows, :] = (_dot_t(x, wct_ref[...]) * _dot_t(x, wht_ref[...])).astype(BF16)

        lane = lax.broadcasted_iota(jnp.int32, cos.shape, 1)
        mq = jnp.where(lane < QK_NOPE, 1.0,
                       jnp.where(lane < QK_NOPE + QK_ROPE, cos, sin)) * q_scale
        mq2 = jnp.concatenate([mq, mq], axis=1)
        qn = _rms_norm(_dot_t(x, wqat_ref[...]), qg_ref[...]).astype(BF16)
        for hp in range(N_HEADS // 2):
            sl = slice(hp * 2 * HEAD_PAD, (hp + 1) * 2 * HEAD_PAD)
            q_ref[rows, sl] = (_dot(qn, wqb_ref[:, sl]) * mq2).astype(BF16)

        kr2 = _dot_t(x, wkrt_ref[...])
        rk = kr2[:, :HEAD_PAD] * cos + kr2[:, HEAD_PAD:] * sin
        rk2 = jnp.concatenate([rk, rk], axis=1)
        kvn = _rms_norm(_dot_t(x, wkvat_ref[...]), kvg_ref[...]).astype(BF16)
        for hp in range(N_HEADS // 2):
            sl = slice(hp * 2 * HEAD_PAD, (hp + 1) * 2 * HEAD_PAD)
            k_ref[rows, sl] = (_dot(kvn, wk_ref[:, sl]) + rk2).astype(BF16)

        vt_ref[0, :, rows] = lax.dot_general(wvt_ref[...], kvn, _NT,
                                             preferred_element_type=F32).astype(BF16)


def _proj(l, h, cos_t, sin_t, w, seq, tm, sub, entry_ln=None):
    t, d = h.shape
    nb = seq // tm
    hq = N_HEADS * HEAD_PAD
    hv = N_HEADS * V_HEAD
    row = lambda width: pl.BlockSpec((tm, width), lambda i: (i, 0))
    ln_args = () if entry_ln is None else tuple(entry_ln)
    ln_specs = [_const_spec(a.shape) for a in ln_args]
    weights = ((w["w_main_t"], (d, d), OFF_B // d), (w["w_main_t"], (d, d), OFF_C // d),
               (w["w_main_t"], (Q_LORA, d), OFF_H // Q_LORA),
               (w["w_main_t"], (KV_LORA, d), OFF_QA // KV_LORA),
               (w["wkr_t"], (2 * HEAD_PAD, d), 0), (w["qg"], (1, Q_LORA), 0),
               (w["wqb"], (Q_LORA, hq), 0), (w["kvg"], (1, KV_LORA), 0),
               (w["wk"], (KV_LORA, hq), 0), (w["wvt"], (hv, KV_LORA), 0))
    q_scale = (QK_NOPE + QK_ROPE) ** -0.5 * math.log2(math.e)
    return pl.pallas_call(
        functools.partial(_proj_kernel, q_scale, sub, entry_ln is not None),
        out_shape=(jax.ShapeDtypeStruct((t, d), BF16),
                   jax.ShapeDtypeStruct((t, hq), BF16),
                   jax.ShapeDtypeStruct((t, hq), BF16),
                   jax.ShapeDtypeStruct((t // seq, hv, seq), BF16)),
        grid=(t // tm,),
        in_specs=[row(d)] + ln_specs + [row(HEAD_PAD), row(HEAD_PAD)]
        + [_layer_spec(l, blk, cb) for _, blk, cb in weights],
        out_specs=(row(d), row(hq), row(hq),
                   pl.BlockSpec((1, hv, tm), lambda i: (i // nb, 0, i % nb))),
        compiler_params=_params(),
        name="proj",
    )(h, *ln_args, cos_t, sin_t, *[a for a, _, _ in weights])


def _row_max8(x):
    while x.shape[0] > SUBLANES:
        half = x.shape[0] // 2
        x = jnp.maximum(x[:half], x[half:])
    return x


def _attn_tile(kc, lag, row0, q_ref, k_ref, vt_ref, y_ref, s_buf, p_bufs):
    seq = k_ref.shape[0]
    tq = s_buf.shape[2]
    nc = seq // kc
    ring = s_buf.shape[0]
    rows = pl.ds(row0, tq)
    ones = jnp.ones((BF16_SUBLANES, kc), BF16)
    chunk_max, parts, outs, q_head = {}, {}, {}, {}

    def score_step(g, h, c):
        if c == 0:
            q_head.clear()
            q_head[h] = q_ref[rows, h * HEAD_PAD:(h + 1) * HEAD_PAD]
        k = k_ref[c * kc:(c + 1) * kc, h * HEAD_PAD:(h + 1) * HEAD_PAD]
        st = lax.dot_general(k, q_head[h], _NT, preferred_element_type=F32)
        s_buf[g % ring] = st
        chunk_max[g] = jnp.max(_row_max8(st), axis=0, keepdims=True)

    p_rows = pl.ds(pl.multiple_of(jnp.minimum(pl.program_id(0), 0), BF16_SUBLANES), kc)

    def exp_step(g, h, c):
        p_bufs[g % ring][p_rows, :] = jnp.exp2(s_buf[g % ring] - chunk_max[g]).astype(BF16)

    def value_step(g, h, c):
        v1 = jnp.concatenate(
            [vt_ref[0, h * V_HEAD:(h + 1) * V_HEAD, c * kc:(c + 1) * kc], ones], axis=0)
        parts[g] = _dot(v1, p_bufs[g % ring][p_rows, :])
        if c == nc - 1:
            gs = range(g - nc + 1, g + 1)
            m = functools.reduce(jnp.maximum, [chunk_max[i] for i in gs])
            ot = sum(parts.pop(i) * jnp.exp2(chunk_max.pop(i) - m) for i in gs)
            outs[h] = ot[:V_HEAD] / ot[V_HEAD:V_HEAD + 1]
            if h % 2 == 1:
                o2 = jnp.concatenate([outs.pop(h - 1), outs.pop(h)], axis=0)
                y_ref[rows, (h - 1) * V_HEAD:(h + 1) * V_HEAD] = o2.T.astype(BF16)

    total = N_HEADS * nc
    for step in range(total + 2 * lag):
        for stage, fn in enumerate((score_step, exp_step, value_step)):
            g = step - stage * lag
            if 0 <= g < total:
                fn(g, g // nc, g % nc)


def _attn_kernel(kc, lag, q_ref, k_ref, vt_ref, y_ref, s_buf, *p_bufs):
    tq = s_buf.shape[2]

    def tile(j, carry):
        _attn_tile(kc, lag, pl.multiple_of(j * tq, tq), q_ref, k_ref, vt_ref, y_ref,
                   s_buf, p_bufs)
        return carry

    lax.fori_loop(0, q_ref.shape[0] // tq, tile, 0)


def _attention(q, k, vt, seq, tq, n_sub, kc, lag):
    t, hq = q.shape
    tb = tq * n_sub
    nb = seq // tb
    hv = N_HEADS * V_HEAD
    ring = lag + 1
    return pl.pallas_call(
        functools.partial(_attn_kernel, kc, lag),
        out_shape=jax.ShapeDtypeStruct((t, hv), BF16),
        grid=(t // tb,),
        in_specs=[pl.BlockSpec((tb, hq), lambda i: (i, 0)),
                  pl.BlockSpec((seq, hq), lambda i: (i // nb, 0)),
                  pl.BlockSpec((1, hv, seq), lambda i: (i // nb, 0, 0))],
        out_specs=pl.BlockSpec((tb, hv), lambda i: (i, 0)),
        scratch_shapes=[pltpu.VMEM((ring, kc, tq), F32)] + [pltpu.VMEM((kc, tq), BF16)] * ring,
        compiler_params=_params(),
        name="attention",
    )(q, k, vt)


def _mix_kernel(alpha, nb, sub, entry_norm, h_ref, *refs):
    if entry_norm:
        lng_ref, lnb_ref, *refs = refs
    (u_ref, up_ref, un_ref, y_ref, wbt_ref, wgt_ref, bg_ref, cw_ref, wco_ref, wmo_ref,
     wo_ref, lg_ref, lb_ref, o_ref) = refs
    i = pl.program_id(0)
    tm = h_ref.shape[0]
    halo = up_ref.shape[0]
    has_prev = (i % nb != 0).astype(F32)
    has_next = (i % nb != nb - 1).astype(F32)
    cw = cw_ref[...]
    for j in range(tm // sub):
        lo, hi = j * sub, (j + 1) * sub
        hres = h_ref[lo:hi, :]
        if entry_norm:
            hres = _layer_norm(hres, lng_ref[...], lnb_ref[...])
        x = hres.astype(BF16)

        u = u_ref[lo:hi, :].astype(F32)
        if j == 0:
            prev_row = up_ref[halo - 1:halo, :].astype(F32) * has_prev
        else:
            prev_row = u_ref[lo - 1:lo, :].astype(F32)
        if hi == tm:
            next_row = un_ref[0:1, :].astype(F32) * has_next
        else:
            next_row = u_ref[hi:hi + 1, :].astype(F32)
        r = lax.broadcasted_iota(jnp.int32, u.shape, 0)
        u_m1 = jnp.where(r == 0, prev_row, pltpu.roll(u, 1, axis=0))
        u_p1 = jnp.where(r == sub - 1, next_row, pltpu.roll(u, sub - 1, axis=0))
        conv = cw[0:1, :] * u_m1 + cw[1:2, :] * u + cw[2:3, :] * u_p1

        yc = _dot((_dot_t(x, wbt_ref[...]) * conv).astype(BF16), wco_ref[...])
        ym = _dot(y_ref[lo:hi, :], wmo_ref[...])
        g = jax.nn.sigmoid(_dot_t(x, wgt_ref[...]) + bg_ref[...])
        merged = (g[:, :D_MODEL] * yc + g[:, D_MODEL:] * ym).astype(BF16)
        mix = _dot(merged, wo_ref[...])
        o_ref[lo:hi, :] = _layer_norm(alpha * hres + mix, lg_ref[...], lb_ref[...])


def _mix(l, h, u, y, w, alpha, seq, tm, sub, entry_ln=None):
    t, d = h.shape
    nb = seq // tm
    halo = BF16_SUBLANES
    hb = tm // halo
    nhalo = t // halo
    row = lambda width: pl.BlockSpec((tm, width), lambda i: (i, 0))
    ln_args = () if entry_ln is None else tuple(entry_ln)
    ln_specs = [_const_spec(a.shape) for a in ln_args]
    weights = ((w["w_main_t"], (d, d), 0), (w["wg_t"], (2 * d, d), 0), (w["bgate"], (1, 2 * d), 0),
               (w["convw"], (3, d), 0), (w["wco"], (d, d), 0), (w["wmo"], (d, d), 0),
               (w["wout"], (d, d), 0), (w["lmg"], (1, d), 0), (w["lmb"], (1, d), 0))
    return pl.pallas_call(
        functools.partial(_mix_kernel, alpha, nb, sub, entry_ln is not None),
        out_shape=jax.ShapeDtypeStruct((t, d), F32),
        grid=(t // tm,),
        in_specs=[row(d)] + ln_specs + [row(d),
                  pl.BlockSpec((halo, d), lambda i: (jnp.maximum(i * hb - 1, 0), 0)),
                  pl.BlockSpec((halo, d), lambda i: (jnp.minimum((i + 1) * hb, nhalo - 1), 0)),
                  row(d)] + [_layer_spec(l, blk, cb) for _, blk, cb in weights],
        out_specs=row(d),
        compiler_params=_params(),
        name="mix",
    )(h, *ln_args, u, u, u, y, *[a for a, _, _ in weights])


def _mlp_kernel(alpha, n_chunks, h_ref, wu_ref, wd_ref, lg_ref, lb_ref, o_ref):
    hres = h_ref[...]
    x = hres.astype(BF16)
    ck = D_FF // n_chunks
    acc = None
    for c in range(n_chunks):
        a = jnp.maximum(_dot(x, wu_ref[:, c * ck:(c + 1) * ck]), 0.0)
        part = _dot((a * a).astype(BF16), wd_ref[c * ck:(c + 1) * ck, :])
        acc = part if acc is None else acc + part
    o_ref[...] = _layer_norm(alpha * hres + acc, lg_ref[...], lb_ref[...])


def _mlp(l, h, w, alpha, tm):
    t, d = h.shape
    row = pl.BlockSpec((tm, d), lambda i: (i, 0))
    weights = ((w["wup"], (d, D_FF)), (w["wdown"], (D_FF, d)), (w["lfg"], (1, d)),
               (w["lfb"], (1, d)))
    return pl.pallas_call(
        functools.partial(_mlp_kernel, alpha, 4),
        out_shape=jax.ShapeDtypeStruct((t, d), F32),
        grid=(t // tm,),
        in_specs=[row] + [_layer_spec(l, blk) for _, blk in weights],
        out_specs=row,
        compiler_params=_params(),
        name="mlp",
    )(h, *[a for a, _ in weights])


def _rot_cols(w):
    half = QK_ROPE // 2
    return jnp.concatenate([-w[..., half:], w[..., :half]], axis=-1)


def _prep(w_in, b_gate, conv_w, w_conv_out, q_norm_g, w_q_b, kv_norm_g, w_kv_b,
          w_mla_o, w_out, ln_mix_g, ln_mix_b, w_up, w_down, ln_ffn_g, ln_ffn_b):
    depth, d = w_in.shape[0], D_MODEL
    w_in_t = jnp.swapaxes(w_in, 1, 2)
    wy_t = w_in_t[:, OFF_KVA:OFF_KR, :]
    wyr_t = jnp.concatenate([-wy_t[:, QK_ROPE // 2:], wy_t[:, :QK_ROPE // 2]], axis=1)
    z64 = jnp.zeros((depth, QK_NOPE, d), F32)
    wkr_t = jnp.concatenate([z64, wy_t, wy_t, z64, wyr_t, wyr_t], axis=1)

    wq = w_q_b.reshape(depth, Q_LORA, N_HEADS, QK_NOPE + QK_ROPE)
    wqb = jnp.concatenate([wq, _rot_cols(wq[..., QK_NOPE:])], axis=-1)
    wqb = wqb.reshape(depth, Q_LORA, N_HEADS * HEAD_PAD)

    wkv = w_kv_b.reshape(depth, KV_LORA, N_HEADS, QK_NOPE + V_HEAD)
    wk = jnp.concatenate([wkv[..., :QK_NOPE], jnp.zeros_like(wkv[..., QK_NOPE:])], axis=-1)
    wk = wk.reshape(depth, KV_LORA, N_HEADS * HEAD_PAD)
    wvt = jnp.swapaxes(wkv[..., QK_NOPE:].reshape(depth, KV_LORA, N_HEADS * V_HEAD), 1, 2)

    bf = lambda a: a.astype(BF16)
    row = lambda a: a.reshape(depth, 1, -1)
    return {
        "w_main_t": bf(w_in_t), "wkr_t": bf(wkr_t),
        "wg_t": bf(w_in_t[:, OFF_KR:, :]), "bgate": row(b_gate),
        "qg": row(q_norm_g), "wqb": bf(wqb), "kvg": row(kv_norm_g),
        "wk": bf(wk), "wvt": bf(wvt),
        "convw": conv_w, "wco": bf(w_conv_out), "wmo": bf(w_mla_o), "wout": bf(w_out),
        "lmg": row(ln_mix_g), "lmb": row(ln_mix_b),
        "wup": bf(w_up), "wdown": bf(w_down),
        "lfg": row(ln_ffn_g), "lfb": row(ln_ffn_b),
    }


def kernel(x, positions, ln_in_g, ln_in_b, w_in, b_gate, conv_w, w_conv_out, q_norm_g, w_q_b, kv_norm_g, w_kv_b, w_mla_o, w_out, ln_mix_g, ln_mix_b, w_up, w_down, ln_ffn_g, ln_ffn_b):
    batch, seq, d = x.shape
    depth = w_in.shape[0]
    t = batch * seq
    alpha = (2 * depth) ** 0.25
    tm = 512
    tq = 256
    kc = 256
    lag = 3

    inv_freq = 1.0 / (ROPE_THETA ** (jnp.arange(0, QK_ROPE, 2, dtype=F32) / QK_ROPE))
    cos_t, sin_t = _rope_tables(positions.reshape(1, t), inv_freq.reshape(-1, 1), tm)

    w = _prep(w_in, b_gate, conv_w, w_conv_out, q_norm_g, w_q_b, kv_norm_g, w_kv_b,
              w_mla_o, w_out, ln_mix_g, ln_mix_b, w_up, w_down, ln_ffn_g, ln_ffn_b)
    h = x.reshape(t, d)
    for l in range(depth):
        entry_ln = (ln_in_g.reshape(1, d), ln_in_b.reshape(1, d)) if l == 0 else None
        u, q, k, vt = _proj(l, h, cos_t, sin_t, w, seq, 2 * tm, 2 * tm, entry_ln)
        y = _attention(q, k, vt, seq, tq, 4, kc, lag)
        h = _mix(l, h, u, y, w, alpha, seq, 2 * tm, 2 * tm, entry_ln)
        h = _mlp(l, h, w, alpha, 2 * tm)
    return h.reshape(batch, seq, d)
```

```python
import functools
import math

import jax
import jax.numpy as jnp
from jax import lax
from jax.experimental import pallas as pl
from jax.experimental.pallas import tpu as pltpu

D_MODEL = 1024
N_HEADS = 16
QK_NOPE = 64
QK_ROPE = 32
V_HEAD = 64
Q_LORA = 768
KV_LORA = 256
ROPE_THETA = 10000.0
D_FF = 4 * D_MODEL
LN_EPS = 1e-5
RMS_EPS = 1e-6
HEAD_PAD = 128
SUBLANES = 8
BF16_SUBLANES = 16

OFF_B = D_MODEL
OFF_C = 2 * D_MODEL
OFF_H = 3 * D_MODEL
OFF_QA = OFF_H + Q_LORA
OFF_KVA = OFF_QA + KV_LORA
OFF_KR = OFF_KVA + QK_ROPE

VMEM_LIMIT_BYTES = 56 * 1024 * 1024

BF16 = jnp.bfloat16
F32 = jnp.float32

_NT = (((1,), (1,)), ((), ()))


def _dot(a, b):
    return jnp.dot(a, b, preferred_element_type=F32)


def _dot_t(a, bt):
    return lax.dot_general(a, bt, _NT, preferred_element_type=F32)


def _layer_norm(z, g, b):
    mu = jnp.mean(z, axis=-1, keepdims=True)
    zc = z - mu
    var = jnp.mean(zc * zc, axis=-1, keepdims=True)
    return zc * lax.rsqrt(var + LN_EPS) * g + b


def _rms_norm(z, g):
    ms = jnp.mean(z * z, axis=-1, keepdims=True)
    return z * lax.rsqrt(ms + RMS_EPS) * g


def _const_spec(shape):
    nd = len(shape)
    return pl.BlockSpec(shape, lambda *_: (0,) * nd, pipeline_mode=pl.Buffered(1))


def _layer_spec(l, block, row_block=0):
    return pl.BlockSpec((None,) + tuple(block), lambda *_: (l, row_block, 0),
                        pipeline_mode=pl.Buffered(1))


def _params():
    return pltpu.CompilerParams(
        dimension_semantics=("arbitrary",), vmem_limit_bytes=VMEM_LIMIT_BYTES)


def _rope_table_kernel(pos_ref, freq_ref, cos_ref, sin_ref):
    ang = freq_ref[...] * pos_ref[...].astype(F32)
    cos = jnp.cos(ang)
    sin = jnp.sin(ang)
    pad = (QK_NOPE, ang.shape[1])
    cos_ref[...] = jnp.concatenate([jnp.ones(pad, F32)] + [cos] * 4, axis=0).T
    sin_ref[...] = jnp.concatenate([jnp.zeros(pad, F32)] + [sin] * 4, axis=0).T


def _rope_tables(pos_row, freq_col, tm):
    t = pos_row.shape[1]
    out = pl.BlockSpec((tm, HEAD_PAD), lambda i: (i, 0))
    return pl.pallas_call(
        _rope_table_kernel,
        out_shape=(jax.ShapeDtypeStruct((t, HEAD_PAD), F32),) * 2,
        grid=(t // tm,),
        in_specs=[pl.BlockSpec((1, tm), lambda i: (0, i)), _const_spec(freq_col.shape)],
        out_specs=(out, out),
        compiler_params=_params(),
        name="rope_tables",
    )(pos_row, freq_col)


def _proj_kernel(q_scale, sub, entry_norm, h_ref, *refs):
    if entry_norm:
        lng_ref, lnb_ref, *refs, h0_ref = refs
    (cos_ref, sin_ref, wct_ref, wht_ref, wqat_ref, wkvat_ref, wkrt_ref, qg_ref, wqb_ref,
     kvg_ref, wk_ref, wvt_ref, u_ref, q_ref, k_ref, vt_ref) = refs
    for j in range(h_ref.shape[0] // sub):
        rows = slice(j * sub, (j + 1) * sub)
        hres = h_ref[rows, :]
        if entry_norm:
            hres = _layer_norm(hres, lng_ref[...], lnb_ref[...])
            h0_ref[rows, :] = hres
        x = hres.astype(BF16)
        cos = cos_ref[rows, :]
        sin = sin_ref[rows, :]

        u_ref[rows, :] = (_dot_t(x, wct_ref[...]) * _dot_t(x, wht_ref[...])).astype(BF16)

        lane = lax.broadcasted_iota(jnp.int32, cos.shape, 1)
        mq = jnp.where(lane < QK_NOPE, 1.0,
                       jnp.where(lane < QK_NOPE + QK_ROPE, cos, sin)) * q_scale
        mq2 = jnp.concatenate([mq, mq], axis=1)
        qn = _rms_norm(_dot_t(x, wqat_ref[...]), qg_ref[...]).astype(BF16)
        for hp in range(N_HEADS // 2):
            sl = slice(hp * 2 * HEAD_PAD, (hp + 1) * 2 * HEAD_PAD)
            q_ref[rows, sl] = (_dot(qn, wqb_ref[:, sl]) * mq2).astype(BF16)

        kr2 = _dot_t(x, wkrt_ref[...])
        rk = kr2[:, :HEAD_PAD] * cos + kr2[:, HEAD_PAD:] * sin
        rk2 = jnp.concatenate([rk, rk], axis=1)
        kvn = _rms_norm(_dot_t(x, wkvat_ref[...]), kvg_ref[...]).astype(BF16)
        for hp in range(N_HEADS // 2):
            sl = slice(hp * 2 * HEAD_PAD, (hp + 1) * 2 * HEAD_PAD)
            k_ref[rows, sl] = (_dot(kvn, wk_ref[:, sl]) + rk2).astype(BF16)

        vt_ref[0, :, rows] = lax.dot_general(wvt_ref[...], kvn, _NT,
                                             preferred_element_type=F32).astype(BF16)


def _proj(l, h, cos_t, sin_t, w, seq, tm, sub, entry_ln=None):
    t, d = h.shape
    nb = seq // tm
    hq = N_HEADS * HEAD_PAD
    hv = N_HEADS * V_HEAD
    row = lambda width: pl.BlockSpec((tm, width), lambda i: (i, 0))
    ln_args = () if entry_ln is None else tuple(entry_ln)
    ln_specs = [_const_spec(a.shape) for a in ln_args]
    extra_shape = () if entry_ln is None else (jax.ShapeDtypeStruct((t, d), F32),)
    extra_spec = () if entry_ln is None else (row(d),)
    weights = ((w["w_main_t"], (d, d), OFF_B // d), (w["w_main_t"], (d, d), OFF_C // d),
               (w["w_main_t"], (Q_LORA, d), OFF_H // Q_LORA),
               (w["w_main_t"], (KV_LORA, d), OFF_QA // KV_LORA),
               (w["wkr_t"], (2 * HEAD_PAD, d), 0), (w["qg"], (1, Q_LORA), 0),
               (w["wqb"], (Q_LORA, hq), 0), (w["kvg"], (1, KV_LORA), 0),
               (w["wk"], (KV_LORA, hq), 0), (w["wvt"], (hv, KV_LORA), 0))
    q_scale = (QK_NOPE + QK_ROPE) ** -0.5 * math.log2(math.e)
    return pl.pallas_call(
        functools.partial(_proj_kernel, q_scale, sub, entry_ln is not None),
        out_shape=(jax.ShapeDtypeStruct((t, d), BF16),
                   jax.ShapeDtypeStruct((t, hq), BF16),
                   jax.ShapeDtypeStruct((t, hq), BF16),
                   jax.ShapeDtypeStruct((t // seq, hv, seq), BF16)) + extra_shape,
        grid=(t // tm,),
        in_specs=[row(d)] + ln_specs + [row(HEAD_PAD), row(HEAD_PAD)]
        + [_layer_spec(l, blk, cb) for _, blk, cb in weights],
        out_specs=(row(d), row(hq), row(hq),
                   pl.BlockSpec((1, hv, tm), lambda i: (i // nb, 0, i % nb))) + extra_spec,
        compiler_params=_params(),
        name="proj",
    )(h, *ln_args, cos_t, sin_t, *[a for a, _, _ in weights])


def _row_max8(x):
    while x.shape[0] > SUBLANES:
        half = x.shape[0] // 2
        x = jnp.maximum(x[:half], x[half:])
    return x


def _attn_tile(kc, lag, row0, q_ref, k_ref, vt_ref, y_ref, s_buf, p_bufs):
    seq = k_ref.shape[0]
    tq = s_buf.shape[2]
    nc = seq // kc
    ring = s_buf.shape[0]
    rows = pl.ds(row0, tq)
    ones = jnp.ones((BF16_SUBLANES, kc), BF16)
    chunk_max, parts, outs, q_head = {}, {}, {}, {}

    def score_step(g, h, c):
        if c == 0:
            q_head.clear()
            q_head[h] = q_ref[rows, h * HEAD_PAD:(h + 1) * HEAD_PAD]
        k = k_ref[c * kc:(c + 1) * kc, h * HEAD_PAD:(h + 1) * HEAD_PAD]
        st = lax.dot_general(k, q_head[h], _NT, preferred_element_type=F32)
        s_buf[g % ring] = st
        chunk_max[g] = jnp.max(_row_max8(st), axis=0, keepdims=True)

    p_rows = pl.ds(pl.multiple_of(jnp.minimum(pl.program_id(0), 0), BF16_SUBLANES), kc)

    def exp_step(g, h, c):
        p_bufs[g % ring][p_rows, :] = jnp.exp2(s_buf[g % ring] - chunk_max[g]).astype(BF16)

    def value_step(g, h, c):
        v1 = jnp.concatenate(
            [vt_ref[0, h * V_HEAD:(h + 1) * V_HEAD, c * kc:(c + 1) * kc], ones], axis=0)
        parts[g] = _dot(v1, p_bufs[g % ring][p_rows, :])
        if c == nc - 1:
            gs = range(g - nc + 1, g + 1)
            m = functools.reduce(jnp.maximum, [chunk_max[i] for i in gs])
            ot = sum(parts.pop(i) * jnp.exp2(chunk_max.pop(i) - m) for i in gs)
            outs[h] = ot[:V_HEAD] / ot[V_HEAD:V_HEAD + 1]
            if h % 2 == 1:
                o2 = jnp.concatenate([outs.pop(h - 1), outs.pop(h)], axis=0)
                y_ref[rows, (h - 1) * V_HEAD:(h + 1) * V_HEAD] = o2.T.astype(BF16)

    total = N_HEADS * nc
    for step in range(total + 2 * lag):
        for stage, fn in enumerate((score_step, exp_step, value_step)):
            g = step - stage * lag
            if 0 <= g < total:
                fn(g, g // nc, g % nc)


def _attn_kernel(kc, lag, q_ref, k_ref, vt_ref, y_ref, s_buf, *p_bufs):
    tq = s_buf.shape[2]

    def tile(j, carry):
        _attn_tile(kc, lag, pl.multiple_of(j * tq, tq), q_ref, k_ref, vt_ref, y_ref,
                   s_buf, p_bufs)
        return carry

    lax.fori_loop(0, q_ref.shape[0] // tq, tile, 0)


def _attention(q, k, vt, seq, tq, n_sub, kc, lag):
    t, hq = q.shape
    tb = tq * n_sub
    nb = seq // tb
    hv = N_HEADS * V_HEAD
    ring = lag + 1
    return pl.pallas_call(
        functools.partial(_attn_kernel, kc, lag),
        out_shape=jax.ShapeDtypeStruct((t, hv), BF16),
        grid=(t // tb,),
        in_specs=[pl.BlockSpec((tb, hq), lambda i: (i, 0)),
                  pl.BlockSpec((seq, hq), lambda i: (i // nb, 0)),
                  pl.BlockSpec((1, hv, seq), lambda i: (i // nb, 0, 0))],
        out_specs=pl.BlockSpec((tb, hv), lambda i: (i, 0)),
        scratch_shapes=[pltpu.VMEM((ring, kc, tq), F32)] + [pltpu.VMEM((kc, tq), BF16)] * ring,
        compiler_params=_params(),
        name="attention",
    )(q, k, vt)


def _mix_kernel(alpha, nb, sub, h_ref, u_ref, up_ref, un_ref, y_ref, wbt_ref, wgt_ref, bg_ref,
                cw_ref, wco_ref, wmo_ref, wo_ref, lg_ref, lb_ref, o_ref):
    i = pl.program_id(0)
    tm = h_ref.shape[0]
    halo = up_ref.shape[0]
    has_prev = (i % nb != 0).astype(F32)
    has_next = (i % nb != nb - 1).astype(F32)
    cw = cw_ref[...]
    for j in range(tm // sub):
        lo, hi = j * sub, (j + 1) * sub
        hres = h_ref[lo:hi, :]
        x = hres.astype(BF16)

        u = u_ref[lo:hi, :].astype(F32)
        if j == 0:
            prev_row = up_ref[halo - 1:halo, :].astype(F32) * has_prev
        else:
            prev_row = u_ref[lo - 1:lo, :].astype(F32)
        if hi == tm:
            next_row = un_ref[0:1, :].astype(F32) * has_next
        else:
            next_row = u_ref[hi:hi + 1, :].astype(F32)
        r = lax.broadcasted_iota(jnp.int32, u.shape, 0)
        u_m1 = jnp.where(r == 0, prev_row, pltpu.roll(u, 1, axis=0))
        u_p1 = jnp.where(r == sub - 1, next_row, pltpu.roll(u, sub - 1, axis=0))
        conv = cw[0:1, :] * u_m1 + cw[1:2, :] * u + cw[2:3, :] * u_p1

        yc = _dot((_dot_t(x, wbt_ref[...]) * conv).astype(BF16), wco_ref[...])
        ym = _dot(y_ref[lo:hi, :], wmo_ref[...])
        g = jax.nn.sigmoid(_dot_t(x, wgt_ref[...]) + bg_ref[...])
        merged = (g[:, :D_MODEL] * yc + g[:, D_MODEL:] * ym).astype(BF16)
        mix = _dot(merged, wo_ref[...])
        o_ref[lo:hi, :] = _layer_norm(alpha * hres + mix, lg_ref[...], lb_ref[...])


def _mix(l, h, u, y, w, alpha, seq, tm, sub):
    t, d = h.shape
    nb = seq // tm
    halo = BF16_SUBLANES
    hb = tm // halo
    nhalo = t // halo
    row = lambda width: pl.BlockSpec((tm, width), lambda i: (i, 0))
    weights = ((w["w_main_t"], (d, d), 0), (w["wg_t"], (2 * d, d), 0), (w["bgate"], (1, 2 * d), 0),
               (w["convw"], (3, d), 0), (w["wco"], (d, d), 0), (w["wmo"], (d, d), 0),
               (w["wout"], (d, d), 0), (w["lmg"], (1, d), 0), (w["lmb"], (1, d), 0))
    return pl.pallas_call(
        functools.partial(_mix_kernel, alpha, nb, sub),
        out_shape=jax.ShapeDtypeStruct((t, d), F32),
        grid=(t // tm,),
        in_specs=[row(d), row(d),
                  pl.BlockSpec((halo, d), lambda i: (jnp.maximum(i * hb - 1, 0), 0)),
                  pl.BlockSpec((halo, d), lambda i: (jnp.minimum((i + 1) * hb, nhalo - 1), 0)),
                  row(d)] + [_layer_spec(l, blk, cb) for _, blk, cb in weights],
        out_specs=row(d),
        compiler_params=_params(),
        name="mix",
    )(h, u, u, u, y, *[a for a, _, _ in weights])


def _mlp_kernel(alpha, n_chunks, h_ref, wu_ref, wd_ref, lg_ref, lb_ref, o_ref):
    hres = h_ref[...]
    x = hres.astype(BF16)
    ck = D_FF // n_chunks
    acc = None
    for c in range(n_chunks):
        a = jnp.maximum(_dot(x, wu_ref[:, c * ck:(c + 1) * ck]), 0.0)
        part = _dot((a * a).astype(BF16), wd_ref[c * ck:(c + 1) * ck, :])
        acc = part if acc is None else acc + part
    o_ref[...] = _layer_norm(alpha * hres + acc, lg_ref[...], lb_ref[...])


def _mlp(l, h, w, alpha, tm):
    t, d = h.shape
    row = pl.BlockSpec((tm, d), lambda i: (i, 0))
    weights = ((w["wup"], (d, D_FF)), (w["wdown"], (D_FF, d)), (w["lfg"], (1, d)),
               (w["lfb"], (1, d)))
    return pl.pallas_call(
        functools.partial(_mlp_kernel, alpha, 4),
        out_shape=jax.ShapeDtypeStruct((t, d), F32),
        grid=(t // tm,),
        in_specs=[row] + [_layer_spec(l, blk) for _, blk in weights],
        out_specs=row,
        compiler_params=_params(),
        name="mlp",
    )(h, *[a for a, _ in weights])


def _mixmlp_kernel(alpha, nb, n_chunks, h_ref, u_ref, up_ref, un_ref, y_ref, wbt_ref, wgt_ref,
                   bg_ref, cw_ref, wco_ref, wmo_ref, wo_ref, lg_ref, lb_ref,
                   wu_ref, wd_ref, fg_ref, fb_ref, o_ref):
    i = pl.program_id(0)
    tm = h_ref.shape[0]
    halo = up_ref.shape[0]
    has_prev = (i % nb != 0).astype(F32)
    has_next = (i % nb != nb - 1).astype(F32)
    hres = h_ref[...]
    x = hres.astype(BF16)

    u = u_ref[...].astype(F32)
    prev_row = up_ref[halo - 1:halo, :].astype(F32) * has_prev
    next_row = un_ref[0:1, :].astype(F32) * has_next
    r = lax.broadcasted_iota(jnp.int32, u.shape, 0)
    u_m1 = jnp.where(r == 0, prev_row, pltpu.roll(u, 1, axis=0))
    u_p1 = jnp.where(r == tm - 1, next_row, pltpu.roll(u, tm - 1, axis=0))
    cw = cw_ref[...]
    conv = cw[0:1, :] * u_m1 + cw[1:2, :] * u + cw[2:3, :] * u_p1

    yc = _dot((_dot_t(x, wbt_ref[...]) * conv).astype(BF16), wco_ref[...])
    ym = _dot(y_ref[...], wmo_ref[...])
    g = jax.nn.sigmoid(_dot_t(x, wgt_ref[...]) + bg_ref[...])
    merged = (g[:, :D_MODEL] * yc + g[:, D_MODEL:] * ym).astype(BF16)
    h1 = _layer_norm(alpha * hres + _dot(merged, wo_ref[...]), lg_ref[...], lb_ref[...])

    x1 = h1.astype(BF16)
    ck = D_FF // n_chunks
    acc = None
    for c in range(n_chunks):
        a = jnp.maximum(_dot(x1, wu_ref[:, c * ck:(c + 1) * ck]), 0.0)
        part = _dot((a * a).astype(BF16), wd_ref[c * ck:(c + 1) * ck, :])
        acc = part if acc is None else acc + part
    o_ref[...] = _layer_norm(alpha * h1 + acc, fg_ref[...], fb_ref[...])


def _mixmlp(l, h, u, y, w, alpha, seq, tm):
    t, d = h.shape
    nb = seq // tm
    halo = BF16_SUBLANES
    hb = tm // halo
    nhalo = t // halo
    row = lambda width: pl.BlockSpec((tm, width), lambda i: (i, 0))
    weights = ((w["w_main_t"], (d, d)), (w["wg_t"], (2 * d, d)), (w["bgate"], (1, 2 * d)),
               (w["convw"], (3, d)), (w["wco"], (d, d)), (w["wmo"], (d, d)),
               (w["wout"], (d, d)), (w["lmg"], (1, d)), (w["lmb"], (1, d)),
               (w["wup"], (d, D_FF)), (w["wdown"], (D_FF, d)), (w["lfg"], (1, d)),
               (w["lfb"], (1, d)))
    return pl.pallas_call(
        functools.partial(_mixmlp_kernel, alpha, nb, 4),
        out_shape=jax.ShapeDtypeStruct((t, d), F32),
        grid=(t // tm,),
        in_specs=[row(d), row(d),
                  pl.BlockSpec((halo, d), lambda i: (jnp.maximum(i * hb - 1, 0), 0)),
                  pl.BlockSpec((halo, d), lambda i: (jnp.minimum((i + 1) * hb, nhalo - 1), 0)),
                  row(d)] + [_layer_spec(l, blk) for _, blk in weights],
        out_specs=row(d),
        compiler_params=_params(),
        name="mixmlp",
    )(h, u, u, u, y, *[a for a, _ in weights])


def _rot_cols(w):
    half = QK_ROPE // 2
    return jnp.concatenate([-w[..., half:], w[..., :half]], axis=-1)


def _prep(w_in, b_gate, conv_w, w_conv_out, q_norm_g, w_q_b, kv_norm_g, w_kv_b,
          w_mla_o, w_out, ln_mix_g, ln_mix_b, w_up, w_down, ln_ffn_g, ln_ffn_b):
    depth, d = w_in.shape[0], D_MODEL
    w_in_t = jnp.swapaxes(w_in, 1, 2)
    wy_t = w_in_t[:, OFF_KVA:OFF_KR, :]
    wyr_t = jnp.concatenate([-wy_t[:, QK_ROPE // 2:], wy_t[:, :QK_ROPE // 2]], axis=1)
    z64 = jnp.zeros((depth, QK_NOPE, d), F32)
    wkr_t = jnp.concatenate([z64, wy_t, wy_t, z64, wyr_t, wyr_t], axis=1)

    wq = w_q_b.reshape(depth, Q_LORA, N_HEADS, QK_NOPE + QK_ROPE)
    wqb = jnp.concatenate([wq, _rot_cols(wq[..., QK_NOPE:])], axis=-1)
    wqb = wqb.reshape(depth, Q_LORA, N_HEADS * HEAD_PAD)

    wkv = w_kv_b.reshape(depth, KV_LORA, N_HEADS, QK_NOPE + V_HEAD)
    wk = jnp.concatenate([wkv[..., :QK_NOPE], jnp.zeros_like(wkv[..., QK_NOPE:])], axis=-1)
    wk = wk.reshape(depth, KV_LORA, N_HEADS * HEAD_PAD)
    wvt = jnp.swapaxes(wkv[..., QK_NOPE:].reshape(depth, KV_LORA, N_HEADS * V_HEAD), 1, 2)

    bf = lambda a: a.astype(BF16)
    row = lambda a: a.reshape(depth, 1, -1)
    return {
        "w_main_t": bf(w_in_t), "wkr_t": bf(wkr_t),
        "wg_t": bf(w_in_t[:, OFF_KR:, :]), "bgate": row(b_gate),
        "qg": row(q_norm_g), "wqb": bf(wqb), "kvg": row(kv_norm_g),
        "wk": bf(wk), "wvt": bf(wvt),
        "convw": conv_w, "wco": bf(w_conv_out), "wmo": bf(w_mla_o), "wout": bf(w_out),
        "lmg": row(ln_mix_g), "lmb": row(ln_mix_b),
        "wup": bf(w_up), "wdown": bf(w_down),
        "lfg": row(ln_ffn_g), "lfb": row(ln_ffn_b),
    }


def kernel(x, positions, ln_in_g, ln_in_b, w_in, b_gate, conv_w, w_conv_out, q_norm_g, w_q_b, kv_norm_g, w_kv_b, w_mla_o, w_out, ln_mix_g, ln_mix_b, w_up, w_down, ln_ffn_g, ln_ffn_b):
    batch, seq, d = x.shape
    depth = w_in.shape[0]
    t = batch * seq
    alpha = (2 * depth) ** 0.25
    tm = 512
    tq = 256
    kc = 256
    lag = 3

    inv_freq = 1.0 / (ROPE_THETA ** (jnp.arange(0, QK_ROPE, 2, dtype=F32) / QK_ROPE))
    cos_t, sin_t = _rope_tables(positions.reshape(1, t), inv_freq.reshape(-1, 1), tm)

    w = _prep(w_in, b_gate, conv_w, w_conv_out, q_norm_g, w_q_b, kv_norm_g, w_kv_b,
              w_mla_o, w_out, ln_mix_g, ln_mix_b, w_up, w_down, ln_ffn_g, ln_ffn_b)
    h = x.reshape(t, d)
    for l in range(depth):
        if l == 0:
            u, q, k, vt, h = _proj(l, h, cos_t, sin_t, w, seq, tm, tm,
                                   entry_ln=(ln_in_g.reshape(1, d), ln_in_b.reshape(1, d)))
        else:
            u, q, k, vt = _proj(l, h, cos_t, sin_t, w, seq, 2 * tm, 2 * tm)
        y = _attention(q, k, vt, seq, tq, 4, kc, lag)
        h = _mixmlp(l, h, u, y, w, alpha, seq, tm)
    return h.reshape(batch, seq, d)
```
